```python
import math
import jax, jax.numpy as jnp
from jax import lax
import numpy as np

D_MODEL = 1024
BATCH = 8
SEQ = 4096
DEPTH = 1

CHUNK = 64
Q_BLOCK = 128
A_HEADS = 8
A_HEAD_DIM = 64
A_WIDTH = A_HEADS * A_HEAD_DIM
IDX_HEADS = 8
IDX_DIM = 64
TOPK_MAX = 256
REL_BUCKETS = 32
REL_MAX_DIST = 128
B_WIDTH = 512
CONV_K = 31
PLE_DIM = 256
EPS = 1e-6
IN_SIZES = (A_WIDTH, A_WIDTH, A_WIDTH, A_WIDTH,
            IDX_HEADS * IDX_DIM, IDX_DIM, IDX_HEADS,
            2 * B_WIDTH, B_WIDTH,
            D_MODEL, D_MODEL)
IN_WIDTH = 4 * A_WIDTH + IDX_HEADS * IDX_DIM + IDX_DIM + IDX_HEADS + 3 * B_WIDTH + 2 * D_MODEL

kernel_name = "hybrid_dsa_conformer_gated_block"


def rms_norm(x, g):
    xf = x.astype(jnp.float32)
    y = xf * lax.rsqrt(jnp.mean(xf * xf, axis=-1, keepdims=True) + EPS)
    return (y * g.astype(jnp.float32)).astype(x.dtype)


def layer_norm(x, g, b):
    xf = x.astype(jnp.float32)
    mu = jnp.mean(xf, axis=-1, keepdims=True)
    xc = xf - mu
    var = jnp.mean(xc * xc, axis=-1, keepdims=True)
    y = xc * lax.rsqrt(var + EPS) * g.astype(jnp.float32) + b.astype(jnp.float32)
    return y.astype(x.dtype)


def t5_bucket(rel):
    half = REL_BUCKETS // 2
    max_exact = half // 2
    base = jnp.where(rel > 0, half, 0).astype(jnp.int32)
    n = jnp.abs(rel)
    nf = jnp.maximum(n, 1).astype(jnp.float32)
    large = max_exact + (jnp.log(nf / max_exact) / math.log(REL_MAX_DIST / max_exact)
                         * (half - max_exact)).astype(jnp.int32)
    large = jnp.minimum(large, half - 1)
    return base + jnp.where(n < max_exact, n, large)


def sparse_attention(q, k, v, q_idx, k_idx, w_idx, rel_table):
    bsz, seq = q.shape[0], q.shape[1]
    k_top = min(TOPK_MAX, seq // 4)
    n_blocks = seq // Q_BLOCK
    pos = jnp.arange(seq, dtype=jnp.int32)
    key_chunk = pos // CHUNK
    idx_scale = (IDX_DIM ** -0.5) * (IDX_HEADS ** -0.5)
    attn_scale = A_HEAD_DIM ** -0.5
    gather = jax.vmap(lambda a, i: a[i])

    def to_blocks(a):
        return a.reshape(bsz, n_blocks, Q_BLOCK, *a.shape[2:]).swapaxes(0, 1)

    def block_fn(args):
        qb, qib, wib, qpos = args
        dots = jnp.einsum('bqhd,bkd->bqhk', qib, k_idx).astype(jnp.float32)
        scores = jnp.einsum('bqhk,bqh->bqk', jax.nn.relu(dots),
                            wib.astype(jnp.float32) * idx_scale)
        admissible = key_chunk[None, :] <= (qpos // CHUNK)[:, None]
        scores = jnp.where(admissible[None], scores, -jnp.inf)
        top_val, top_idx = lax.top_k(scores, k_top)
        valid = jnp.isfinite(top_val)
        k_sel = gather(k, top_idx)
        v_sel = gather(v, top_idx)
        logits = jnp.einsum('bqhd,bqkhd->bqhk', qb, k_sel).astype(jnp.float32) * attn_scale
        rel = top_idx - qpos[None, :, None]
        bias = rel_table.astype(jnp.float32)[t5_bucket(rel)]
        logits = logits + jnp.moveaxis(bias, -1, 2)
        logits = jnp.where(valid[:, :, None, :], logits, -jnp.inf)
        probs = jax.nn.softmax(logits, axis=-1).astype(v.dtype)
        return jnp.einsum('bqhk,bqkhd->bqhd', probs, v_sel)

    out = lax.map(block_fn, (to_blocks(q), to_blocks(q_idx), to_blocks(w_idx),
                             pos.reshape(n_blocks, Q_BLOCK)))
    return out.swapaxes(0, 1).reshape(bsz, seq, A_HEADS, A_HEAD_DIM)


def causal_depthwise_conv(u, w, b):
    y = lax.conv_general_dilated(u, w.astype(u.dtype), window_strides=(1,),
                                 padding=[(CONV_K - 1, 0)],
                                 dimension_numbers=('NWC', 'WIO', 'NWC'),
                                 feature_group_count=u.shape[-1])
    return y + b.astype(u.dtype)


def setup_inputs(seed: int = 0) -> dict:
    key = jax.random.key(seed)
    ks = jax.random.split(key, 18)
    f32 = jnp.float32
    nrm = lambda k, shape, scale: jax.random.normal(k, shape, f32) * scale
    return {
        "x": nrm(ks[0], (BATCH, SEQ, D_MODEL), 1.0),
        "p": nrm(ks[1], (DEPTH, BATCH, SEQ, PLE_DIM), 1.0),
        "norm_in_g": 1.0 + nrm(ks[2], (DEPTH, D_MODEL), 0.02),
        "w_in": nrm(ks[3], (DEPTH, D_MODEL, IN_WIDTH), D_MODEL ** -0.5),
        "conv_w": nrm(ks[4], (DEPTH, CONV_K, 1, B_WIDTH), CONV_K ** -0.5),
        "conv_b": nrm(ks[5], (DEPTH, B_WIDTH), 0.02),
        "conv_ln_g": 1.0 + nrm(ks[6], (DEPTH, B_WIDTH), 0.02),
        "conv_ln_b": nrm(ks[7], (DEPTH, B_WIDTH), 0.02),
        "w_branch_a": nrm(ks[8], (DEPTH, A_WIDTH, D_MODEL), A_WIDTH ** -0.5),
        "w_branch_b": nrm(ks[9], (DEPTH, B_WIDTH, D_MODEL), B_WIDTH ** -0.5),
        "w_out": nrm(ks[10], (DEPTH, D_MODEL, D_MODEL), D_MODEL ** -0.5),
        "ple_norm_g": 1.0 + nrm(ks[11], (DEPTH, D_MODEL), 0.02),
        "w_ple_gate": nrm(ks[12], (DEPTH, D_MODEL, D_MODEL), D_MODEL ** -0.5),
        "w_ple_proj": nrm(ks[13], (DEPTH, PLE_DIM, D_MODEL), PLE_DIM ** -0.5),
        "rel_bias": nrm(ks[14], (REL_BUCKETS, A_HEADS), 0.5),
        "final_norm_g": 1.0 + nrm(ks[15], (D_MODEL,), 0.02),
    }


def reference(x, p, norm_in_g, w_in, conv_w, conv_b, conv_ln_g, conv_ln_b,
              w_branch_a, w_branch_b, w_out, ple_norm_g, w_ple_gate, w_ple_proj,
              rel_bias, final_norm_g):
    bsz, seq, _ = x.shape
    offsets = list(np.cumsum(IN_SIZES)[:-1])
    for i in range(DEPTH):
        h = rms_norm(x, norm_in_g[i])
        proj = h @ w_in[i]
        (q, k, v, z_a, q_idx, k_idx, w_idx, glu_in, z_b,
         gate_a, gate_b) = jnp.split(proj, offsets, axis=-1)
        heads = lambda t: t.reshape(bsz, seq, A_HEADS, A_HEAD_DIM)
        attn = sparse_attention(heads(q), heads(k), heads(v),
                                q_idx.reshape(bsz, seq, IDX_HEADS, IDX_DIM),
                                k_idx, w_idx, rel_bias)
        y_a = (attn.reshape(bsz, seq, A_WIDTH) * jax.nn.silu(z_a)) @ w_branch_a[i]
        u = glu_in[..., :B_WIDTH] * jax.nn.sigmoid(glu_in[..., B_WIDTH:])
        c = causal_depthwise_conv(u, conv_w[i], conv_b[i])
        c = jax.nn.silu(layer_norm(c, conv_ln_g[i], conv_ln_b[i]))
        y_b = (c * jax.nn.silu(z_b)) @ w_branch_b[i]
        merged = jax.nn.sigmoid(gate_a) * y_a + jax.nn.sigmoid(gate_b) * y_b
        x = x + merged @ w_out[i]
        e = p[i] @ w_ple_proj[i]
        g = jax.nn.sigmoid(rms_norm(x, ple_norm_g[i]) @ w_ple_gate[i])
        x = x + g * e
    return rms_norm(x, final_norm_g)
```

```python
import functools
import math

import numpy as np
import jax
import jax.numpy as jnp
from jax import lax
from jax.experimental import pallas as pl
from jax.experimental.pallas import tpu as pltpu

D_MODEL = 1024
CHUNK = 64
CHUNK_SHIFT = 6
A_HEADS = 8
A_HEAD_DIM = 64
A_WIDTH = A_HEADS * A_HEAD_DIM
IDX_HEADS = 8
IDX_DIM = 64
TOPK_MAX = 256
REL_BUCKETS = 32
REL_MAX_DIST = 128
B_WIDTH = 512
CONV_K = 31
PLE_DIM = 256
EPS = 1e-6
IN_SIZES = (A_WIDTH, A_WIDTH, A_WIDTH, A_WIDTH, IDX_HEADS * IDX_DIM, IDX_DIM, IDX_HEADS,
            2 * B_WIDTH, B_WIDTH, D_MODEL, D_MODEL)

LANES = 128
TQ = 128
KEY_CHUNK = 512
NEAR = 2 * TQ
FAR_BUCKET = 15
CONV_HALO = 32
VMEM_LIMIT = 48 * 1024 * 1024

INT_MIN = -2 ** 31
INT_MAX = 2 ** 31 - 1
KEY_NEG_INF = (0xFF800000 ^ 0x7FFFFFFF) - 2 ** 32
KEY_POS_INF = 0x7F800000
M_INIT = -3.0e38

F32 = jnp.float32
BF16 = jnp.bfloat16
I32 = jnp.int32
DN_LAST = (((1,), (1,)), ((), ()))


def _rms(x, g):
    return x * lax.rsqrt(jnp.mean(x * x, axis=-1, keepdims=True) + EPS) * g


def _sigmoid(x):
    return 1.0 / (1.0 + jnp.exp(-x))


def _proj_kernel(x_ref, g_ref, wnat_ref, wew_ref, wvt_ref, wwt_ref, nat_ref, ew_ref, vt_ref, wt_ref):
    h = _rms(x_ref[...], g_ref[...]).astype(BF16)
    nat_ref[...] = jnp.dot(h, wnat_ref[...], preferred_element_type=F32).astype(BF16)
    ew_ref[...] = jnp.dot(h, wew_ref[...], preferred_element_type=F32)
    vt = lax.dot_general(wvt_ref[...], h, DN_LAST, preferred_element_type=F32).astype(BF16)
    for c in range(vt_ref.shape[0]):
        vt_ref[c] = vt[:, c * LANES:(c + 1) * LANES]
    wt_ref[...] = lax.dot_general(wwt_ref[...], h, DN_LAST, preferred_element_type=F32)


def _proj(x, g, wnat, wew, wvt, wwt, tm):
    bsz, seq, d = x.shape
    n_nat, n_ew = wnat.shape[1], wew.shape[1]
    full = lambda a: pl.BlockSpec(a.shape, lambda b, s: (0,) * a.ndim)
    return pl.pallas_call(
        _proj_kernel,
        grid=(bsz, seq // tm),
        in_specs=[pl.BlockSpec((None, tm, d), lambda b, s: (b, s, 0)),
                  full(g), full(wnat), full(wew), full(wvt), full(wwt)],
        out_specs=[pl.BlockSpec((None, tm, n_nat), lambda b, s: (b, s, 0)),
                   pl.BlockSpec((None, tm, n_ew), lambda b, s: (b, s, 0)),
                   pl.BlockSpec((None, tm // LANES, A_WIDTH, LANES), lambda b, s: (b, s, 0, 0)),
                   pl.BlockSpec((None, 16, tm), lambda b, s: (b, 0, s))],
        out_shape=[jax.ShapeDtypeStruct((bsz, seq, n_nat), BF16),
                   jax.ShapeDtypeStruct((bsz, seq, n_ew), F32),
                   jax.ShapeDtypeStruct((bsz, seq // LANES, A_WIDTH, LANES), BF16),
                   jax.ShapeDtypeStruct((bsz, 16, seq), F32)],
        compiler_params=pltpu.CompilerParams(dimension_semantics=("arbitrary", "arbitrary"),
                                             vmem_limit_bytes=VMEM_LIMIT),
        name="in_proj",
    )(x, g, wnat, wew, wvt, wwt)


def _conv_kernel(glu_ref, zb_ref, cw_ref, cb_ref, lg_ref, lb_ref, out_ref, ext_ref, *, row_tile):
    s = pl.program_id(1)
    ts = glu_ref.shape[0]

    @pl.when(s == 0)
    def _():
        ext_ref[0:CONV_HALO, :] = jnp.zeros((CONV_HALO, B_WIDTH), F32)

    @pl.when(s > 0)
    def _():
        ext_ref[0:CONV_HALO, :] = ext_ref[ts:ts + CONV_HALO, :]

    glu = glu_ref[...]
    ext_ref[CONV_HALO:CONV_HALO + ts, :] = glu[:, :B_WIDTH] * _sigmoid(glu[:, B_WIDTH:])
    first = CONV_HALO - (CONV_K - 1)
    for r in range(ts // row_tile):
        r0 = r * row_tile
        acc = jnp.zeros((row_tile, B_WIDTH), F32)
        for j in range(CONV_K):
            acc = acc + ext_ref[r0 + first + j:r0 + first + j + row_tile, :] * cw_ref[j:j + 1, :]
        acc = acc + cb_ref[...]
        mu = jnp.mean(acc, axis=-1, keepdims=True)
        xc = acc - mu
        var = jnp.mean(xc * xc, axis=-1, keepdims=True)
        y = xc * lax.rsqrt(var + EPS) * lg_ref[...] + lb_ref[...]
        zb = zb_ref[r0:r0 + row_tile, :]
        out_ref[r0:r0 + row_tile, :] = ((y * _sigmoid(y)) * (zb * _sigmoid(zb))).astype(BF16)


def _conv_branch(ew, cw, cb, lg, lb, ts):
    bsz, seq, _ = ew.shape
    full = lambda a: pl.BlockSpec(a.shape, lambda b, s: (0,) * a.ndim)
    return pl.pallas_call(
        functools.partial(_conv_kernel, row_tile=64),
        grid=(bsz, seq // ts),
        in_specs=[pl.BlockSpec((None, ts, 2 * B_WIDTH), lambda b, s: (b, s, 0)),
                  pl.BlockSpec((None, ts, B_WIDTH), lambda b, s: (b, s, 3)),
                  full(cw), full(cb), full(lg), full(lb)],
        out_specs=pl.BlockSpec((None, ts, B_WIDTH), lambda b, s: (b, s, 0)),
        out_shape=jax.ShapeDtypeStruct((bsz, seq, B_WIDTH), BF16),
        scratch_shapes=[pltpu.VMEM((ts + CONV_HALO, B_WIDTH), F32)],
        compiler_params=pltpu.CompilerParams(dimension_semantics=("arbitrary", "arbitrary"),
                                             vmem_limit_bytes=VMEM_LIMIT),
        name="conv_branch",
    )(ew, ew, cw, cb, lg, lb)


def _f32_key(x):
    b = lax.bitcast_convert_type(x, I32)
    return b ^ ((b >> 31) & 0x7FFFFFFF)


def _attn_kernel(relb_ref, bucket_ref, q_ref, qi_ref, wt_ref, kidx_ref, k_ref, vt_ref, out_ref,
                 key_scr, mb_scr, mbh_scr, bias_scr, bidx_scr, bq_scr, ot_scr, *, k_top, idx_bits):
    i = pl.program_id(1)
    n_chunks = i // (KEY_CHUNK // TQ) + 1
    idx_scale = (IDX_DIM ** -0.5) * (IDX_HEADS ** -0.5)
    attn_scale = A_HEAD_DIM ** -0.5

    @pl.when(i == 0)
    def _init_bias():
        for d in range(2):
            for h in range(A_HEADS):
                bias_scr[d, h] = jnp.zeros((TQ, LANES), F32) + relb_ref[0, h]

        def fill(b, carry):
            for d in range(2):
                hit = bucket_ref[d] == b
                for h in range(A_HEADS):
                    bias_scr[d, h] = jnp.where(hit, relb_ref[b, h], bias_scr[d, h])
            return carry

        lax.fori_loop(1, REL_BUCKETS, fill, 0)
        for d in range(2):
            for h in range(A_HEADS):
                bias_scr[d, h] = bias_scr[d, h] - relb_ref[FAR_BUCKET, h]

    lane = lax.broadcasted_iota(I32, (TQ, LANES), 1)
    low = lane < A_HEAD_DIM
    for j in range(A_HEADS // 2):
        qi = qi_ref[:, j * LANES:(j + 1) * LANES].astype(F32)
        bidx_scr[j, 0:TQ, :] = jnp.where(low, qi, 0.0).astype(BF16)
        bidx_scr[j, TQ:2 * TQ, :] = jnp.where(low, 0.0, qi).astype(BF16)
        qq = q_ref[:, j * LANES:(j + 1) * LANES].astype(F32) * attn_scale
        bq_scr[j, 0:TQ, :] = jnp.where(low, qq, 0.0).astype(BF16)
        bq_scr[j, TQ:2 * TQ, :] = jnp.where(low, 0.0, qq).astype(BF16)

    q_chunk = (i * TQ + lax.broadcasted_iota(I32, (1, LANES), 1)) >> CHUNK_SHIFT
    w_scaled = [wt_ref[h:h + 1, :] * idx_scale for h in range(IDX_HEADS)]
    row_iota = lax.broadcasted_iota(I32, (KEY_CHUNK, LANES), 0)

    def score_chunk(ci, carry):
        base = pl.multiple_of(ci * KEY_CHUNK, KEY_CHUNK)
        kc = kidx_ref[pl.ds(base, KEY_CHUNK), :]
        acc = jnp.zeros((KEY_CHUNK, LANES), F32)
        for j in range(IDX_HEADS // 2):
            d = lax.dot_general(kc, bidx_scr[j], DN_LAST, preferred_element_type=F32)
            acc = acc + jnp.maximum(d[:, :TQ], 0.0) * w_scaled[2 * j]
            acc = acc + jnp.maximum(d[:, TQ:], 0.0) * w_scaled[2 * j + 1]
        admissible = ((base + row_iota) >> CHUNK_SHIFT) <= q_chunk
        key_scr[pl.ds(base, KEY_CHUNK), :] = _f32_key(jnp.where(admissible, acc, -jnp.inf))
        return carry

    lax.fori_loop(0, n_chunks, score_chunk, 0)

    def count_where(pred):
        def body(ci, a):
            base = pl.multiple_of(ci * KEY_CHUNK, KEY_CHUNK)
            hit = pred(key_scr[pl.ds(base, KEY_CHUNK), :], base).astype(I32)
            return a + jnp.sum(hit.reshape(KEY_CHUNK // 8, 8, LANES), axis=0)
        a = lax.fori_loop(0, n_chunks, body, jnp.zeros((8, LANES), I32))
        return jnp.sum(a, axis=0, keepdims=True)

    def bisect(p, carry):
        c, cnt_c = carry
        trial = c + lax.shift_left(jnp.int32(1), 31 - p)
        cnt = count_where(lambda blk, base: blk >= trial)
        ok = cnt >= k_top
        return jnp.where(ok, trial, c), jnp.where(ok, cnt, cnt_c)

    thr, cnt_thr = lax.fori_loop(
        0, 32, bisect,
        (jnp.full((1, LANES), INT_MIN, I32), jnp.zeros((1, LANES), I32) + n_chunks * KEY_CHUNK))

    def write_mask(select):
        def body(ci, carry):
            base = pl.multiple_of(ci * KEY_CHUNK, KEY_CHUNK)
            blk = key_scr[pl.ds(base, KEY_CHUNK), :]
            sel = select(blk, base) & (blk > KEY_NEG_INF) & (blk < KEY_POS_INF)
            mb_scr[pl.ds(base, KEY_CHUNK), :] = jnp.where(sel, 0.0, -jnp.inf)
            return carry
        lax.fori_loop(0, n_chunks, body, 0)

    write_mask(lambda blk, base: blk >= thr)

    tie = (cnt_thr > k_top) & (thr > KEY_NEG_INF) & (thr < KEY_POS_INF)

    @pl.when(jnp.max(tie.astype(I32)) > 0)
    def _break_ties():
        need = k_top - count_where(lambda blk, base: blk > thr)

        def grow(p, last):
            trial = last + lax.shift_left(jnp.int32(1), idx_bits - 1 - p)
            below = count_where(lambda blk, base: (blk == thr) & ((base + row_iota) < trial))
            return jnp.where(below <= need - 1, trial, last)

        last = lax.fori_loop(0, idx_bits, grow, jnp.zeros((1, LANES), I32))
        last = jnp.where(tie, last, INT_MAX)
        write_mask(lambda blk, base: (blk > thr) | ((blk == thr) & ((base + row_iota) <= last)))

    near0 = pl.multiple_of(jnp.maximum(i - 1, 0) * TQ, TQ)
    near_blk = jnp.maximum(i - 1, 0)
    first_tile = jnp.minimum(i, 1)
    mb_near = mb_scr[pl.ds(near0, NEAR), :]
    for h in range(A_HEADS):
        mbh_scr[h, 0:TQ, :] = mb_near[0:TQ] + bias_scr[first_tile, h]
        mbh_scr[h, TQ:NEAR, :] = mb_near[TQ:NEAR] + bias_scr[0, h]
    mb_scr[pl.ds(near0, NEAR), :] = jnp.full((NEAR, LANES), -jnp.inf, F32)
    n_far_chunks = (near_blk + KEY_CHUNK // TQ - 1) // (KEY_CHUNK // TQ)

    def attend(carry, kc, masks, vts, bq):
        s = lax.dot_general(kc, bq, DN_LAST, preferred_element_type=F32)
        out = []
        for hh in range(2):
            m, l, acc = carry[hh]
            sh = s[:, hh * TQ:(hh + 1) * TQ] + masks[hh]
            m_new = jnp.maximum(m, jnp.max(sh, axis=0, keepdims=True))
            alpha = jnp.exp(m - m_new)
            p = jnp.exp(sh - m_new)
            l = alpha * l + jnp.sum(p, axis=0, keepdims=True)
            pb = p.astype(BF16)
            pv = jnp.zeros((A_HEAD_DIM, TQ), F32)
            for t, vt in enumerate(vts[hh]):
                pv = pv + jnp.dot(vt, pb[t * TQ:(t + 1) * TQ, :], preferred_element_type=F32)
            out.append((m_new, l, alpha * acc + pv))
        return tuple(out)

    for j in range(A_HEADS // 2):
        bq = bq_scr[j]
        rows = [slice((2 * j + hh) * A_HEAD_DIM, (2 * j + hh + 1) * A_HEAD_DIM) for hh in range(2)]

        def far_chunk(ci, carry):
            base = pl.multiple_of(ci * KEY_CHUNK, KEY_CHUNK)
            kc = k_ref[pl.ds(base, KEY_CHUNK), j * LANES:(j + 1) * LANES]
            mb = mb_scr[pl.ds(base, KEY_CHUNK), :]
            blk0 = ci * (KEY_CHUNK // TQ)
            vts = [[vt_ref[blk0 + t, rows[hh], :] for t in range(KEY_CHUNK // TQ)] for hh in range(2)]
            return attend(carry, kc, (mb, mb), vts, bq)

        init = tuple((jnp.full((1, TQ), M_INIT, F32), jnp.zeros((1, TQ), F32),
                      jnp.zeros((A_HEAD_DIM, TQ), F32)) for _ in range(2))
        carry = lax.fori_loop(0, n_far_chunks, far_chunk, init)
        kc = k_ref[pl.ds(near0, NEAR), j * LANES:(j + 1) * LANES]
        vts = [[vt_ref[near_blk + t, rows[hh], :] for t in range(NEAR // TQ)] for hh in range(2)]
        carry = attend(carry, kc, (mbh_scr[2 * j], mbh_scr[2 * j + 1]), vts, bq)
        for hh in range(2):
            _, l, acc = carry[hh]
            ot_scr[rows[hh], :] = acc / l

    out_ref[...] = ot_scr[...].T


def _t5_bucket(rel):
    half = REL_BUCKETS // 2
    max_exact = half // 2
    base = jnp.where(rel > 0, half, 0).astype(jnp.int32)
    n = jnp.abs(rel)
    nf = jnp.maximum(n, 1).astype(jnp.float32)
    large = max_exact + (jnp.log(nf / max_exact) / math.log(REL_MAX_DIST / max_exact)
                         * (half - max_exact)).astype(jnp.int32)
    large = jnp.minimum(large, half - 1)
    return base + jnp.where(n < max_exact, n, large)


def _attention(nat, wt, vt, rel_bias):
    bsz, seq, _ = nat.shape
    k_top = min(TOPK_MAX, seq // 4)
    idx_bits = max(1, int(math.ceil(math.log2(seq))))
    r = jnp.arange(TQ, dtype=jnp.int32)[:, None]
    t = jnp.arange(TQ, dtype=jnp.int32)[None, :]
    buckets = jnp.stack([_t5_bucket(r - t - TQ * d) for d in range(2)])
    return pl.pallas_call(
        functools.partial(_attn_kernel, k_top=k_top, idx_bits=idx_bits),
        grid=(bsz, seq // TQ),
        in_specs=[pl.BlockSpec(memory_space=pltpu.SMEM),
                  pl.BlockSpec((2, TQ, LANES), lambda b, i: (0, 0, 0)),
                  pl.BlockSpec((None, TQ, A_WIDTH), lambda b, i: (b, i, 0)),
                  pl.BlockSpec((None, TQ, A_WIDTH), lambda b, i: (b, i, 2)),
                  pl.BlockSpec((None, 16, TQ), lambda b, i: (b, 0, i)),
                  pl.BlockSpec((None, seq, LANES), lambda b, i: (b, 0, 3 * A_WIDTH // LANES)),
                  pl.BlockSpec((None, seq, A_WIDTH), lambda b, i: (b, 0, 1)),
                  pl.BlockSpec((None, seq // LANES, A_WIDTH, LANES), lambda b, i: (b, 0, 0, 0))],
        out_specs=pl.BlockSpec((None, TQ, A_WIDTH), lambda b, i: (b, i, 0)),
        out_shape=jax.ShapeDtypeStruct((bsz, seq, A_WIDTH), F32),
        scratch_shapes=[pltpu.VMEM((seq, LANES), I32),
                        pltpu.VMEM((seq, LANES), F32),
                        pltpu.VMEM((A_HEADS, NEAR, LANES), F32),
                        pltpu.VMEM((2, A_HEADS, TQ, LANES), F32),
                        pltpu.VMEM((A_HEADS // 2, 2 * TQ, LANES), BF16),
                        pltpu.VMEM((A_HEADS // 2, 2 * TQ, LANES), BF16),
                        pltpu.VMEM((A_WIDTH, TQ), F32)],
        compiler_params=pltpu.CompilerParams(dimension_semantics=("arbitrary", "arbitrary"),
                                             vmem_limit_bytes=VMEM_LIMIT),
        name="sparse_attn",
    )(rel_bias, buckets, nat, nat, wt, nat, nat, vt)


def _out_kernel(attn_ref, za_ref, cb_ref, ga_ref, gb_ref, x_ref, p_ref, wa_ref, wb_ref, wo_ref,
                pg_ref, wpg_ref, wpp_ref, fg_ref, out_ref):
    za = za_ref[...]
    ya_in = (attn_ref[...] * (za * _sigmoid(za))).astype(BF16)
    y_a = jnp.dot(ya_in, wa_ref[...], preferred_element_type=F32)
    y_b = jnp.dot(cb_ref[...], wb_ref[...], preferred_element_type=F32)
    merged = _sigmoid(ga_ref[...]) * y_a + _sigmoid(gb_ref[...]) * y_b
    x1 = x_ref[...] + jnp.dot(merged.astype(BF16), wo_ref[...], preferred_element_type=F32)
    e = jnp.dot(p_ref[...].astype(BF16), wpp_ref[...], preferred_element_type=F32)
    gate = _sigmoid(jnp.dot(_rms(x1, pg_ref[...]).astype(BF16), wpg_ref[...], preferred_element_type=F32))
    out_ref[...] = _rms(x1 + gate * e, fg_ref[...])


def _out_stage(attn, ew, cbr, x, p, wa, wb, wo, pg, wpg, wpp, fg, tm):
    bsz, seq, d = x.shape
    full = lambda a: pl.BlockSpec(a.shape, lambda b, s: (0,) * a.ndim)
    return pl.pallas_call(
        _out_kernel,
        grid=(bsz, seq // tm),
        in_specs=[pl.BlockSpec((None, tm, A_WIDTH), lambda b, s: (b, s, 0)),
                  pl.BlockSpec((None, tm, A_WIDTH), lambda b, s: (b, s, 2)),
                  pl.BlockSpec((None, tm, B_WIDTH), lambda b, s: (b, s, 0)),
                  pl.BlockSpec((None, tm, D_MODEL), lambda b, s: (b, s, 2)),
                  pl.BlockSpec((None, tm, D_MODEL), lambda b, s: (b, s, 3)),
                  pl.BlockSpec((None, tm, d), lambda b, s: (b, s, 0)),
                  pl.BlockSpec((None, tm, PLE_DIM), lambda b, s: (b, s, 0)),
                  full(wa), full(wb), full(wo), full(pg), full(wpg), full(wpp), full(fg)],
        out_specs=pl.BlockSpec((None, tm, d), lambda b, s: (b, s, 0)),
        out_shape=jax.ShapeDtypeStruct((bsz, seq, d), F32),
        compiler_params=pltpu.CompilerParams(dimension_semantics=("arbitrary", "arbitrary"),
                                             vmem_limit_bytes=VMEM_LIMIT),
        name="merge_out",
    )(attn, ew, cbr, ew, ew, x, p, wa, wb, wo, pg, wpg, wpp, fg)


def kernel(x, p, norm_in_g, w_in, conv_w, conv_b, conv_ln_g, conv_ln_b, w_branch_a, w_branch_b, w_out,
           ple_norm_g, w_ple_gate, w_ple_proj, rel_bias, final_norm_g):
    seq = x.shape[1]
    assert w_in.shape[0] == 1, "one layer: the fused output stage ends with the final norm"
    assert seq % KEY_CHUNK == 0 and seq >= 2 * TQ and x.shape[2] == D_MODEL
    offs = np.cumsum((0,) + IN_SIZES)
    row = lambda v: v.reshape(1, -1)
    for i in range(1):
        w = w_in[i]
        wq, wk, wv, wza, wqi, wki, wwi, wglu, wzb, wga, wgb = [w[:, offs[n]:offs[n + 1]] for n in range(11)]
        wnat = jnp.concatenate([wq, wk, wqi, wki, wki], axis=1).astype(BF16)
        wew = jnp.concatenate([wglu, wza, wzb, wga, wgb], axis=1).astype(BF16)
        wvt = wv.T.astype(BF16)
        wwt = jnp.pad(wwi.T, ((0, 16 - IDX_HEADS), (0, 0))).astype(BF16)
        nat, ew, vt, wt = _proj(x, row(norm_in_g[i]), wnat, wew, wvt, wwt, tm=256)
        attn = _attention(nat, wt, vt, rel_bias)
        cbr = _conv_branch(ew, conv_w[i].reshape(CONV_K, B_WIDTH), row(conv_b[i]), row(conv_ln_g[i]),
                           row(conv_ln_b[i]), ts=512)
        x = _out_stage(attn, ew, cbr, x, p[i], w_branch_a[i].astype(BF16), w_branch_b[i].astype(BF16),
                       w_out[i].astype(BF16), row(ple_norm_g[i]), w_ple_gate[i].astype(BF16),
                       w_ple_proj[i].astype(BF16), row(final_norm_g), tm=256)
    return x
```

```python
import functools
import math

import numpy as np
import jax
import jax.numpy as jnp
from jax import lax
from jax.experimental import pallas as pl
from jax.experimental.pallas import tpu as pltpu

D_MODEL = 1024
CHUNK = 64
CHUNK_SHIFT = 6
A_HEADS = 8
A_HEAD_DIM = 64
A_WIDTH = A_HEADS * A_HEAD_DIM
IDX_HEADS = 8
IDX_DIM = 64
TOPK_MAX = 256
REL_BUCKETS = 32
REL_MAX_DIST = 128
B_WIDTH = 512
CONV_K = 31
PLE_DIM = 256
EPS = 1e-6
IN_SIZES = (A_WIDTH, A_WIDTH, A_WIDTH, A_WIDTH, IDX_HEADS * IDX_DIM, IDX_DIM, IDX_HEADS,
            2 * B_WIDTH, B_WIDTH, D_MODEL, D_MODEL)

LANES = 128
TQ = 128
KEY_CHUNK = 512
NEAR = 2 * TQ
ACC_ROWS = A_HEAD_DIM + 16
FAR_BUCKET = 15
CONV_HALO = 32
VMEM_LIMIT = 48 * 1024 * 1024

INT_MIN = -2 ** 31
INT_MAX = 2 ** 31 - 1
KEY_NEG_INF = (0xFF800000 ^ 0x7FFFFFFF) - 2 ** 32
KEY_POS_INF = 0x7F800000
I16_MIN = -2 ** 15
I16_MAX = 2 ** 15 - 1
PACK = 16
M_INIT = -3.0e38

F32 = jnp.float32
BF16 = jnp.bfloat16
I32 = jnp.int32
I16 = jnp.int16
DN_LAST = (((1,), (1,)), ((), ()))


def _rms(x, g):
    return x * lax.rsqrt(jnp.mean(x * x, axis=-1, keepdims=True) + EPS) * g


def _sigmoid(x):
    return 1.0 / (1.0 + jnp.exp(-x))


def _proj_kernel(x_ref, g_ref, wnat_ref, wew_ref, wvt_ref, wwt_ref, nat_ref, ew_ref, vt_ref, wt_ref):
    h = _rms(x_ref[...], g_ref[...]).astype(BF16)
    nat_ref[...] = jnp.dot(h, wnat_ref[...], preferred_element_type=F32).astype(BF16)
    ew_ref[...] = jnp.dot(h, wew_ref[...], preferred_element_type=F32)
    vt = lax.dot_general(wvt_ref[...], h, DN_LAST, preferred_element_type=F32).astype(BF16)
    for c in range(vt_ref.shape[0]):
        vt_ref[c] = vt[:, c * LANES:(c + 1) * LANES]
    wt_ref[...] = lax.dot_general(wwt_ref[...], h, DN_LAST, preferred_element_type=F32)


def _proj(x, g, wnat, wew, wvt, wwt, tm):
    bsz, seq, d = x.shape
    n_nat, n_ew = wnat.shape[1], wew.shape[1]
    full = lambda a: pl.BlockSpec(a.shape, lambda b, s: (0,) * a.ndim)
    return pl.pallas_call(
        _proj_kernel,
        grid=(bsz, seq // tm),
        in_specs=[pl.BlockSpec((None, tm, d), lambda b, s: (b, s, 0)),
                  full(g), full(wnat), full(wew), full(wvt), full(wwt)],
        out_specs=[pl.BlockSpec((None, tm, n_nat), lambda b, s: (b, s, 0)),
                   pl.BlockSpec((None, tm, n_ew), lambda b, s: (b, s, 0)),
                   pl.BlockSpec((None, tm // LANES, A_WIDTH, LANES), lambda b, s: (b, s, 0, 0)),
                   pl.BlockSpec((None, 16, tm), lambda b, s: (b, 0, s))],
        out_shape=[jax.ShapeDtypeStruct((bsz, seq, n_nat), BF16),
                   jax.ShapeDtypeStruct((bsz, seq, n_ew), F32),
                   jax.ShapeDtypeStruct((bsz, seq // LANES, A_WIDTH, LANES), BF16),
                   jax.ShapeDtypeStruct((bsz, 16, seq), F32)],
        compiler_params=pltpu.CompilerParams(dimension_semantics=("arbitrary", "arbitrary"),
                                             vmem_limit_bytes=VMEM_LIMIT),
        name="in_proj",
    )(x, g, wnat, wew, wvt, wwt)


def _conv_kernel(glu_ref, zb_ref, cw_ref, cb_ref, lg_ref, lb_ref, out_ref, ext_ref, *, row_tile):
    s = pl.program_id(1)
    ts = glu_ref.shape[0]

    @pl.when(s == 0)
    def _():
        ext_ref[0:CONV_HALO, :] = jnp.zeros((CONV_HALO, B_WIDTH), F32)

    @pl.when(s > 0)
    def _():
        ext_ref[0:CONV_HALO, :] = ext_ref[ts:ts + CONV_HALO, :]

    glu = glu_ref[...]
    ext_ref[CONV_HALO:CONV_HALO + ts, :] = glu[:, :B_WIDTH] * _sigmoid(glu[:, B_WIDTH:])
    first = CONV_HALO - (CONV_K - 1)
    for r in range(ts // row_tile):
        r0 = r * row_tile
        acc = jnp.zeros((row_tile, B_WIDTH), F32)
        for j in range(CONV_K):
            acc = acc + ext_ref[r0 + first + j:r0 + first + j + row_tile, :] * cw_ref[j:j + 1, :]
        acc = acc + cb_ref[...]
        mu = jnp.mean(acc, axis=-1, keepdims=True)
        xc = acc - mu
        var = jnp.mean(xc * xc, axis=-1, keepdims=True)
        y = xc * lax.rsqrt(var + EPS) * lg_ref[...] + lb_ref[...]
        zb = zb_ref[r0:r0 + row_tile, :]
        out_ref[r0:r0 + row_tile, :] = ((y * _sigmoid(y)) * (zb * _sigmoid(zb))).astype(BF16)


def _conv_branch(ew, cw, cb, lg, lb, ts):
    bsz, seq, _ = ew.shape
    full = lambda a: pl.BlockSpec(a.shape, lambda b, s: (0,) * a.ndim)
    return pl.pallas_call(
        functools.partial(_conv_kernel, row_tile=64),
        grid=(bsz, seq // ts),
        in_specs=[pl.BlockSpec((None, ts, 2 * B_WIDTH), lambda b, s: (b, s, 0)),
                  pl.BlockSpec((None, ts, B_WIDTH), lambda b, s: (b, s, 3)),
                  full(cw), full(cb), full(lg), full(lb)],
        out_specs=pl.BlockSpec((None, ts, B_WIDTH), lambda b, s: (b, s, 0)),
        out_shape=jax.ShapeDtypeStruct((bsz, seq, B_WIDTH), BF16),
        scratch_shapes=[pltpu.VMEM((ts + CONV_HALO, B_WIDTH), F32)],
        compiler_params=pltpu.CompilerParams(dimension_semantics=("arbitrary", "arbitrary"),
                                             vmem_limit_bytes=VMEM_LIMIT),
        name="conv_branch",
    )(ew, ew, cw, cb, lg, lb)


def _f32_key(x):
    b = lax.bitcast_convert_type(x, I32)
    return b ^ ((b >> 31) & 0x7FFFFFFF)


def _attn_kernel(relb_ref, bucket_ref, q_ref, qi_ref, wt_ref, kidx_ref, k_ref, vt_ref, out_ref,
                 key_scr, hi_scr, lo_scr, mb_scr, mbh_scr, bias_scr, bidx_scr, bq_scr, m_scr, acc_scr, ot_scr, *, k_top, idx_bits):
    i = pl.program_id(1)
    n_chunks = i // (KEY_CHUNK // TQ) + 1
    idx_scale = (IDX_DIM ** -0.5) * (IDX_HEADS ** -0.5)
    attn_scale = A_HEAD_DIM ** -0.5

    @pl.when(i == 0)
    def _init_bias():
        for d in range(2):
            for h in range(A_HEADS):
                bias_scr[d, h] = jnp.zeros((TQ, LANES), F32) + relb_ref[0, h]

        def fill(b, carry):
            for d in range(2):
                hit = bucket_ref[d] == b
                for h in range(A_HEADS):
                    bias_scr[d, h] = jnp.where(hit, relb_ref[b, h], bias_scr[d, h])
            return carry

        lax.fori_loop(1, REL_BUCKETS, fill, 0)
        for d in range(2):
            for h in range(A_HEADS):
                bias_scr[d, h] = bias_scr[d, h] - relb_ref[FAR_BUCKET, h]

    lane = lax.broadcasted_iota(I32, (TQ, LANES), 1)
    low = lane < A_HEAD_DIM
    for j in range(A_HEADS // 2):
        qi = qi_ref[:, j * LANES:(j + 1) * LANES].astype(F32)
        bidx_scr[j, 0:TQ, :] = jnp.where(low, qi, 0.0).astype(BF16)
        bidx_scr[j, TQ:2 * TQ, :] = jnp.where(low, 0.0, qi).astype(BF16)
        qq = q_ref[:, j * LANES:(j + 1) * LANES].astype(F32) * attn_scale
        bq_scr[j, 0:TQ, :] = jnp.where(low, qq, 0.0).astype(BF16)
        bq_scr[j, TQ:2 * TQ, :] = jnp.where(low, 0.0, qq).astype(BF16)

    q_chunk = (i * TQ + lax.broadcasted_iota(I32, (1, LANES), 1)) >> CHUNK_SHIFT
    w_scaled = [wt_ref[h:h + 1, :] * idx_scale for h in range(IDX_HEADS)]
    row_iota = lax.broadcasted_iota(I32, (KEY_CHUNK, LANES), 0)

    def score_chunk(ci, carry):
        base = pl.multiple_of(ci * KEY_CHUNK, KEY_CHUNK)
        kc = kidx_ref[pl.ds(base, KEY_CHUNK), :]
        acc = jnp.zeros((KEY_CHUNK, LANES), F32)
        for j in range(IDX_HEADS // 2):
            d = lax.dot_general(kc, bidx_scr[j], DN_LAST, preferred_element_type=F32)
            acc = acc + jnp.maximum(d[:, :TQ], 0.0) * w_scaled[2 * j]
            acc = acc + jnp.maximum(d[:, TQ:], 0.0) * w_scaled[2 * j + 1]
        admissible = ((base + row_iota) >> CHUNK_SHIFT) <= q_chunk
        key = _f32_key(jnp.where(admissible, acc, -jnp.inf))
        key_scr[pl.ds(base, KEY_CHUNK), :] = key
        hi_scr[pl.ds(base, KEY_CHUNK), :] = (key >> 16).astype(I16)
        lo_scr[pl.ds(base, KEY_CHUNK), :] = (key ^ 0x8000).astype(I16)
        return carry

    lax.fori_loop(0, n_chunks, score_chunk, 0)

    def count_where(pred):
        def body(ci, a):
            base = pl.multiple_of(ci * KEY_CHUNK, KEY_CHUNK)
            hit = pred(key_scr[pl.ds(base, KEY_CHUNK), :], base).astype(I32)
            return a + jnp.sum(hit.reshape(KEY_CHUNK // 8, 8, LANES), axis=0)
        a = lax.fori_loop(0, n_chunks, body, jnp.zeros((8, LANES), I32))
        return jnp.sum(a, axis=0, keepdims=True)

    def count_half(half_ref, trial):
        t16 = jnp.broadcast_to(trial.astype(I16), (PACK, LANES))
        one, zero = jnp.ones((PACK, LANES), I16), jnp.zeros((PACK, LANES), I16)

        def body(ci, accs):
            base = pl.multiple_of(ci * KEY_CHUNK, KEY_CHUNK)
            accs = list(accs)
            chunk = half_ref[pl.ds(base, KEY_CHUNK), :]
            for r in range(KEY_CHUNK // PACK):
                hit = jnp.where(chunk[r * PACK:(r + 1) * PACK] >= t16, one, zero)
                accs[r % len(accs)] = accs[r % len(accs)] + hit
            return tuple(accs)

        accs = lax.fori_loop(0, n_chunks, body, (zero,) * 4)
        total = (accs[0] + accs[1]) + (accs[2] + accs[3])
        return jnp.sum(total.astype(I32), axis=0, keepdims=True)

    def largest_with(half_ref, need, n_all):
        def step(p, carry):
            d, n_d = carry
            trial = d + lax.shift_left(jnp.int32(1), 15 - p)
            cnt = count_half(half_ref, trial)
            ok = cnt >= need
            return jnp.where(ok, trial, d), jnp.where(ok, cnt, n_d)
        return lax.fori_loop(0, 16, step, (jnp.full((1, LANES), I16_MIN, I32), n_all))

    n_rows = jnp.zeros((1, LANES), I32) + n_chunks * KEY_CHUNK
    top, n_ge_top = largest_with(hi_scr, k_top, n_rows)
    n_gt_top = jnp.where(top == I16_MAX, 0, count_half(hi_scr, jnp.minimum(top + 1, I16_MAX)))

    top16 = jnp.broadcast_to(top.astype(I16), (KEY_CHUNK, LANES))

    def keep_top_group(ci, carry):
        rows = pl.ds(pl.multiple_of(ci * KEY_CHUNK, KEY_CHUNK), KEY_CHUNK)
        lo_scr[rows, :] = jnp.where(hi_scr[rows, :] == top16, lo_scr[rows, :], jnp.int16(I16_MIN))
        return carry

    lax.fori_loop(0, n_chunks, keep_top_group, 0)
    low, n_ge_low = largest_with(lo_scr, k_top - n_gt_top, n_ge_top - n_gt_top)
    thr = (top << 16) | (low - I16_MIN)
    cnt_thr = n_gt_top + n_ge_low

    def write_mask(select):
        def body(ci, carry):
            base = pl.multiple_of(ci * KEY_CHUNK, KEY_CHUNK)
            blk = key_scr[pl.ds(base, KEY_CHUNK), :]
            sel = select(blk, base) & (blk > KEY_NEG_INF) & (blk < KEY_POS_INF)
            mb_scr[pl.ds(base, KEY_CHUNK), :] = jnp.where(sel, 0.0, -jnp.inf)
            return carry
        lax.fori_loop(0, n_chunks, body, 0)

    write_mask(lambda blk, base: blk >= thr)

    tie = (cnt_thr > k_top) & (thr > KEY_NEG_INF) & (thr < KEY_POS_INF)

    @pl.when(jnp.max(tie.astype(I32)) > 0)
    def _break_ties():
        need = k_top - count_where(lambda blk, base: blk > thr)

        def grow(p, last):
            trial = last + lax.shift_left(jnp.int32(1), idx_bits - 1 - p)
            below = count_where(lambda blk, base: (blk == thr) & ((base + row_iota) < trial))
            return jnp.where(below <= need - 1, trial, last)

        last = lax.fori_loop(0, idx_bits, grow, jnp.zeros((1, LANES), I32))
        last = jnp.where(tie, last, INT_MAX)
        write_mask(lambda blk, base: (blk > thr) | ((blk == thr) & ((base + row_iota) <= last)))

    near0 = pl.multiple_of(jnp.maximum(i - 1, 0) * TQ, TQ)
    near_blk = jnp.maximum(i - 1, 0)
    first_tile = jnp.minimum(i, 1)
    mb_near = mb_scr[pl.ds(near0, NEAR), :]
    for h in range(A_HEADS):
        mbh_scr[h, 0:TQ, :] = mb_near[0:TQ] + bias_scr[first_tile, h]
        mbh_scr[h, TQ:NEAR, :] = mb_near[TQ:NEAR] + bias_scr[0, h]
    mb_scr[pl.ds(near0, NEAR), :] = jnp.full((NEAR, LANES), -jnp.inf, F32)
    n_far_chunks = (near_blk + KEY_CHUNK // TQ - 1) // (KEY_CHUNK // TQ)

    m_scr[...] = jnp.full((A_HEADS, TQ), M_INIT, F32)
    acc_scr[...] = jnp.zeros((A_HEADS, ACC_ROWS, TQ), F32)

    def attend(n_rows, key_rows, mask_of, vt_of):
        ones = jnp.ones((ACC_ROWS - A_HEAD_DIM, n_rows), BF16)
        for j in range(A_HEADS // 2):
            s = lax.dot_general(key_rows(j), bq_scr[j], DN_LAST, preferred_element_type=F32)
            for hh in range(2):
                h = 2 * j + hh
                sh = s[:, hh * TQ:(hh + 1) * TQ] + mask_of(h)
                m_old = m_scr[h:h + 1, :]
                m_new = jnp.maximum(m_old, jnp.max(sh, axis=0, keepdims=True))
                p = jnp.exp(sh - m_new).astype(BF16)
                pv = jnp.dot(jnp.concatenate([vt_of(h), ones], axis=0), p, preferred_element_type=F32)
                acc_scr[h] = jnp.exp(m_old - m_new) * acc_scr[h] + pv
                m_scr[h:h + 1, :] = m_new

    def head_rows(h):
        return slice(h * A_HEAD_DIM, (h + 1) * A_HEAD_DIM)

    def far_chunk(ci, carry):
        base = pl.multiple_of(ci * KEY_CHUNK, KEY_CHUNK)
        blk0 = ci * (KEY_CHUNK // TQ)
        mb = mb_scr[pl.ds(base, KEY_CHUNK), :]
        attend(KEY_CHUNK,
               lambda j: k_ref[pl.ds(base, KEY_CHUNK), j * LANES:(j + 1) * LANES],
               lambda h: mb,
               lambda h: jnp.concatenate([vt_ref[blk0 + t, head_rows(h), :] for t in range(KEY_CHUNK // TQ)],
                                         axis=1))
        return carry

    lax.fori_loop(0, n_far_chunks, far_chunk, 0)
    attend(NEAR,
           lambda j: k_ref[pl.ds(near0, NEAR), j * LANES:(j + 1) * LANES],
           lambda h: mbh_scr[h],
           lambda h: jnp.concatenate([vt_ref[near_blk + t, head_rows(h), :] for t in range(NEAR // TQ)], axis=1))
    for h in range(A_HEADS):
        ot_scr[head_rows(h), :] = acc_scr[h, 0:A_HEAD_DIM, :] / acc_scr[h, A_HEAD_DIM:A_HEAD_DIM + 1, :]

    out_ref[...] = ot_scr[...].T


def _t5_bucket(rel):
    half = REL_BUCKETS // 2
    max_exact = half // 2
    base = jnp.where(rel > 0, half, 0).astype(jnp.int32)
    n = jnp.abs(rel)
    nf = jnp.maximum(n, 1).astype(jnp.float32)
    large = max_exact + (jnp.log(nf / max_exact) / math.log(REL_MAX_DIST / max_exact)
                         * (half - max_exact)).astype(jnp.int32)
    large = jnp.minimum(large, half - 1)
    return base + jnp.where(n < max_exact, n, large)


def _attention(nat, wt, vt, rel_bias):
    bsz, seq, _ = nat.shape
    k_top = min(TOPK_MAX, seq // 4)
    idx_bits = max(1, int(math.ceil(math.log2(seq))))
    r = jnp.arange(TQ, dtype=jnp.int32)[:, None]
    t = jnp.arange(TQ, dtype=jnp.int32)[None, :]
    buckets = jnp.stack([_t5_bucket(r - t - TQ * d) for d in range(2)])
    return pl.pallas_call(
        functools.partial(_attn_kernel, k_top=k_top, idx_bits=idx_bits),
        grid=(bsz, seq // TQ),
        in_specs=[pl.BlockSpec(memory_space=pltpu.SMEM),
                  pl.BlockSpec((2, TQ, LANES), lambda b, i: (0, 0, 0)),
                  pl.BlockSpec((None, TQ, A_WIDTH), lambda b, i: (b, i, 0)),
                  pl.BlockSpec((None, TQ, A_WIDTH), lambda b, i: (b, i, 2)),
                  pl.BlockSpec((None, 16, TQ), lambda b, i: (b, 0, i)),
                  pl.BlockSpec((None, seq, LANES), lambda b, i: (b, 0, 3 * A_WIDTH // LANES)),
                  pl.BlockSpec((None, seq, A_WIDTH), lambda b, i: (b, 0, 1)),
                  pl.BlockSpec((None, seq // LANES, A_WIDTH, LANES), lambda b, i: (b, 0, 0, 0))],
        out_specs=pl.BlockSpec((None, TQ, A_WIDTH), lambda b, i: (b, i, 0)),
        out_shape=jax.ShapeDtypeStruct((bsz, seq, A_WIDTH), F32),
        scratch_shapes=[pltpu.VMEM((seq, LANES), I32),
                        pltpu.VMEM((seq, LANES), I16),
                        pltpu.VMEM((seq, LANES), I16),
                        pltpu.VMEM((seq, LANES), F32),
                        pltpu.VMEM((A_HEADS, NEAR, LANES), F32),
                        pltpu.VMEM((2, A_HEADS, TQ, LANES), F32),
                        pltpu.VMEM((A_HEADS // 2, 2 * TQ, LANES), BF16),
                        pltpu.VMEM((A_HEADS // 2, 2 * TQ, LANES), BF16),
                        pltpu.VMEM((A_HEADS, TQ), F32),
                        pltpu.VMEM((A_HEADS, ACC_ROWS, TQ), F32),
                        pltpu.VMEM((A_WIDTH, TQ), F32)],
        compiler_params=pltpu.CompilerParams(dimension_semantics=("arbitrary", "arbitrary"),
                                             vmem_limit_bytes=VMEM_LIMIT),
        name="sparse_attn",
    )(rel_bias, buckets, nat, nat, wt, nat, nat, vt)


def _out_kernel(attn_ref, za_ref, cb_ref, ga_ref, gb_ref, x_ref, p_ref, wa_ref, wb_ref, wo_ref,
                pg_ref, wpg_ref, wpp_ref, fg_ref, out_ref):
    za = za_ref[...]
    ya_in = (attn_ref[...] * (za * _sigmoid(za))).astype(BF16)
    y_a = jnp.dot(ya_in, wa_ref[...], preferred_element_type=F32)
    y_b = jnp.dot(cb_ref[...], wb_ref[...], preferred_element_type=F32)
    merged = _sigmoid(ga_ref[...]) * y_a + _sigmoid(gb_ref[...]) * y_b
    x1 = x_ref[...] + jnp.dot(merged.astype(BF16), wo_ref[...], preferred_element_type=F32)
    e = jnp.dot(p_ref[...].astype(BF16), wpp_ref[...], preferred_element_type=F32)
    gate = _sigmoid(jnp.dot(_rms(x1, pg_ref[...]).astype(BF16), wpg_ref[...], preferred_element_type=F32))
    out_ref[...] = _rms(x1 + gate * e, fg_ref[...])


def _out_stage(attn, ew, cbr, x, p, wa, wb, wo, pg, wpg, wpp, fg, tm):
    bsz, seq, d = x.shape
    full = lambda a: pl.BlockSpec(a.shape, lambda b, s: (0,) * a.ndim)
    return pl.pallas_call(
        _out_kernel,
        grid=(bsz, seq // tm),
        in_specs=[pl.BlockSpec((None, tm, A_WIDTH), lambda b, s: (b, s, 0)),
                  pl.BlockSpec((None, tm, A_WIDTH), lambda b, s: (b, s, 2)),
                  pl.BlockSpec((None, tm, B_WIDTH), lambda b, s: (b, s, 0)),
                  pl.BlockSpec((None, tm, D_MODEL), lambda b, s: (b, s, 2)),
                  pl.BlockSpec((None, tm, D_MODEL), lambda b, s: (b, s, 3)),
                  pl.BlockSpec((None, tm, d), lambda b, s: (b, s, 0)),
                  pl.BlockSpec((None, tm, PLE_DIM), lambda b, s: (b, s, 0)),
                  full(wa), full(wb), full(wo), full(pg), full(wpg), full(wpp), full(fg)],
        out_specs=pl.BlockSpec((None, tm, d), lambda b, s: (b, s, 0)),
        out_shape=jax.ShapeDtypeStruct((bsz, seq, d), F32),
        compiler_params=pltpu.CompilerParams(dimension_semantics=("arbitrary", "arbitrary"),
                                             vmem_limit_bytes=VMEM_LIMIT),
        name="merge_out",
    )(attn, ew, cbr, ew, ew, x, p, wa, wb, wo, pg, wpg, wpp, fg)


def kernel(x, p, norm_in_g, w_in, conv_w, conv_b, conv_ln_g, conv_ln_b, w_branch_a, w_branch_b, w_out,
           ple_norm_g, w_ple_gate, w_ple_proj, rel_bias, final_norm_g):
    seq = x.shape[1]
    assert w_in.shape[0] == 1, "one layer: the fused output stage ends with the final norm"
    assert seq % KEY_CHUNK == 0 and seq >= 2 * TQ and x.shape[2] == D_MODEL
    offs = np.cumsum((0,) + IN_SIZES)
    row = lambda v: v.reshape(1, -1)
    for i in range(1):
        w = w_in[i]
        wq, wk, wv, wza, wqi, wki, wwi, wglu, wzb, wga, wgb = [w[:, offs[n]:offs[n + 1]] for n in range(11)]
        wnat = jnp.concatenate([wq, wk, wqi, wki, wki], axis=1).astype(BF16)
        wew = jnp.concatenate([wglu, wza, wzb, wga, wgb], axis=1).astype(BF16)
        wvt = wv.T.astype(BF16)
        wwt = jnp.pad(wwi.T, ((0, 16 - IDX_HEADS), (0, 0))).astype(BF16)
        nat, ew, vt, wt = _proj(x, row(norm_in_g[i]), wnat, wew, wvt, wwt, tm=256)
        attn = _attention(nat, wt, vt, rel_bias)
        cbr = _conv_branch(ew, conv_w[i].reshape(CONV_K, B_WIDTH), row(conv_b[i]), row(conv_ln_g[i]),
                           row(conv_ln_b[i]), ts=512)
        x = _out_stage(attn, ew, cbr, x, p[i], w_branch_a[i].astype(BF16), w_branch_b[i].astype(BF16),
                       w_out[i].astype(BF16), row(ple_norm_g[i]), w_ple_gate[i].astype(BF16),
                       w_ple_proj[i].astype(BF16), row(final_norm_g), tm=256)
    return x
```

```python
import functools
import math

import numpy as np
import jax
import jax.numpy as jnp
from jax import lax
from jax.experimental import pallas as pl
from jax.experimental.pallas import tpu as pltpu

D_MODEL = 1024
CHUNK = 64
CHUNK_SHIFT = 6
A_HEADS = 8
A_HEAD_DIM = 64
A_WIDTH = A_HEADS * A_HEAD_DIM
IDX_HEADS = 8
IDX_DIM = 64
TOPK_MAX = 256
REL_BUCKETS = 32
REL_MAX_DIST = 128
B_WIDTH = 512
CONV_K = 31
PLE_DIM = 256
EPS = 1e-6
IN_SIZES = (A_WIDTH, A_WIDTH, A_WIDTH, A_WIDTH, IDX_HEADS * IDX_DIM, IDX_DIM, IDX_HEADS,
            2 * B_WIDTH, B_WIDTH, D_MODEL, D_MODEL)

LANES = 128
TQ = 128
KEY_CHUNK = 512
ACC_ROWS = A_HEAD_DIM + 16
FAR_BUCKET = 15
CONV_HALO = 32
VMEM_LIMIT = 48 * 1024 * 1024

INT_MIN = -2 ** 31
INT_MAX = 2 ** 31 - 1
KEY_NEG_INF = (0xFF800000 ^ 0x7FFFFFFF) - 2 ** 32
KEY_POS_INF = 0x7F800000
I16_MIN = -2 ** 15
I16_MAX = 2 ** 15 - 1
PACK = 16
M_INIT = -3.0e38

F32 = jnp.float32
BF16 = jnp.bfloat16
I32 = jnp.int32
I16 = jnp.int16
DN_LAST = (((1,), (1,)), ((), ()))


def _rms(x, g):
    return x * lax.rsqrt(jnp.mean(x * x, axis=-1, keepdims=True) + EPS) * g


def _sigmoid(x):
    return 1.0 / (1.0 + jnp.exp(-x))


def _proj_kernel(x_ref, g_ref, wnat_ref, wew_ref, wvt_ref, wwt_ref, nat_ref, ew_ref, vt_ref, wt_ref):
    h = _rms(x_ref[...], g_ref[...]).astype(BF16)
    nat_ref[...] = jnp.dot(h, wnat_ref[...], preferred_element_type=F32).astype(BF16)
    ew_ref[...] = jnp.dot(h, wew_ref[...], preferred_element_type=F32)
    vt = lax.dot_general(wvt_ref[...], h, DN_LAST, preferred_element_type=F32).astype(BF16)
    for c in range(vt_ref.shape[0]):
        vt_ref[c] = vt[:, c * LANES:(c + 1) * LANES]
    wt_ref[...] = lax.dot_general(wwt_ref[...], h, DN_LAST, preferred_element_type=F32)


def _proj(x, g, wnat, wew, wvt, wwt, tm):
    bsz, seq, d = x.shape
    n_nat, n_ew = wnat.shape[1], wew.shape[1]
    full = lambda a: pl.BlockSpec(a.shape, lambda b, s: (0,) * a.ndim)
    return pl.pallas_call(
        _proj_kernel,
        grid=(bsz, seq // tm),
        in_specs=[pl.BlockSpec((None, tm, d), lambda b, s: (b, s, 0)),
                  full(g), full(wnat), full(wew), full(wvt), full(wwt)],
        out_specs=[pl.BlockSpec((None, tm, n_nat), lambda b, s: (b, s, 0)),
                   pl.BlockSpec((None, tm, n_ew), lambda b, s: (b, s, 0)),
                   pl.BlockSpec((None, tm // LANES, A_WIDTH, LANES), lambda b, s: (b, s, 0, 0)),
                   pl.BlockSpec((None, 16, tm), lambda b, s: (b, 0, s))],
        out_shape=[jax.ShapeDtypeStruct((bsz, seq, n_nat), BF16),
                   jax.ShapeDtypeStruct((bsz, seq, n_ew), F32),
                   jax.ShapeDtypeStruct((bsz, seq // LANES, A_WIDTH, LANES), BF16),
                   jax.ShapeDtypeStruct((bsz, 16, seq), F32)],
        compiler_params=pltpu.CompilerParams(dimension_semantics=("arbitrary", "arbitrary"),
                                             vmem_limit_bytes=VMEM_LIMIT),
        name="in_proj",
    )(x, g, wnat, wew, wvt, wwt)


def _conv_kernel(glu_ref, zb_ref, cw_ref, cb_ref, lg_ref, lb_ref, out_ref, ext_ref, *, row_tile):
    s = pl.program_id(1)
    ts = glu_ref.shape[0]

    @pl.when(s == 0)
    def _():
        ext_ref[0:CONV_HALO, :] = jnp.zeros((CONV_HALO, B_WIDTH), F32)

    @pl.when(s > 0)
    def _():
        ext_ref[0:CONV_HALO, :] = ext_ref[ts:ts + CONV_HALO, :]

    glu = glu_ref[...]
    ext_ref[CONV_HALO:CONV_HALO + ts, :] = glu[:, :B_WIDTH] * _sigmoid(glu[:, B_WIDTH:])
    first = CONV_HALO - (CONV_K - 1)
    for r in range(ts // row_tile):
        r0 = r * row_tile
        acc = jnp.zeros((row_tile, B_WIDTH), F32)
        for j in range(CONV_K):
            acc = acc + ext_ref[r0 + first + j:r0 + first + j + row_tile, :] * cw_ref[j:j + 1, :]
        acc = acc + cb_ref[...]
        mu = jnp.mean(acc, axis=-1, keepdims=True)
        xc = acc - mu
        var = jnp.mean(xc * xc, axis=-1, keepdims=True)
        y = xc * lax.rsqrt(var + EPS) * lg_ref[...] + lb_ref[...]
        zb = zb_ref[r0:r0 + row_tile, :]
        out_ref[r0:r0 + row_tile, :] = ((y * _sigmoid(y)) * (zb * _sigmoid(zb))).astype(BF16)


def _conv_branch(ew, cw, cb, lg, lb, ts):
    bsz, seq, _ = ew.shape
    full = lambda a: pl.BlockSpec(a.shape, lambda b, s: (0,) * a.ndim)
    return pl.pallas_call(
        functools.partial(_conv_kernel, row_tile=64),
        grid=(bsz, seq // ts),
        in_specs=[pl.BlockSpec((None, ts, 2 * B_WIDTH), lambda b, s: (b, s, 0)),
                  pl.BlockSpec((None, ts, B_WIDTH), lambda b, s: (b, s, 3)),
                  full(cw), full(cb), full(lg), full(lb)],
        out_specs=pl.BlockSpec((None, ts, B_WIDTH), lambda b, s: (b, s, 0)),
        out_shape=jax.ShapeDtypeStruct((bsz, seq, B_WIDTH), BF16),
        scratch_shapes=[pltpu.VMEM((ts + CONV_HALO, B_WIDTH), F32)],
        compiler_params=pltpu.CompilerParams(dimension_semantics=("arbitrary", "arbitrary"),
                                             vmem_limit_bytes=VMEM_LIMIT),
        name="conv_branch",
    )(ew, ew, cw, cb, lg, lb)


def _f32_key(x):
    b = lax.bitcast_convert_type(x, I32)
    return b ^ ((b >> 31) & 0x7FFFFFFF)


def _attn_kernel(relb_ref, bucket_ref, q_ref, qi_ref, wt_ref, kidx_ref, k_ref, vt_ref, out_ref,
                 key_scr, hi_scr, lo_scr, mb_scr, mbw_scr, bias_scr, bidx_scr, bq_scr, sa_scr, sb_scr,
                 ca_scr, cb_scr, m_scr, acc_scr, ot_scr, *, k_top, idx_bits):
    i = pl.program_id(1)
    n_chunks = i // (KEY_CHUNK // TQ) + 1
    idx_scale = (IDX_DIM ** -0.5) * (IDX_HEADS ** -0.5)
    attn_scale = A_HEAD_DIM ** -0.5

    @pl.when(i == 0)
    def _init_bias():
        for h in range(A_HEADS):
            bias_scr[2, h] = jnp.zeros((TQ, LANES), F32)
        for d in range(2):
            for h in range(A_HEADS):
                bias_scr[d, h] = jnp.zeros((TQ, LANES), F32) + relb_ref[0, h]

        def fill(b, carry):
            for d in range(2):
                hit = bucket_ref[d] == b
                for h in range(A_HEADS):
                    bias_scr[d, h] = jnp.where(hit, relb_ref[b, h], bias_scr[d, h])
            return carry

        lax.fori_loop(1, REL_BUCKETS, fill, 0)
        for d in range(2):
            for h in range(A_HEADS):
                bias_scr[d, h] = bias_scr[d, h] - relb_ref[FAR_BUCKET, h]

    lane = lax.broadcasted_iota(I32, (TQ, LANES), 1)
    low = lane < A_HEAD_DIM
    for j in range(A_HEADS // 2):
        qi = qi_ref[:, j * LANES:(j + 1) * LANES].astype(F32)
        bidx_scr[j, 0:TQ, :] = jnp.where(low, qi, 0.0).astype(BF16)
        bidx_scr[j, TQ:2 * TQ, :] = jnp.where(low, 0.0, qi).astype(BF16)
        qq = q_ref[:, j * LANES:(j + 1) * LANES].astype(F32) * attn_scale
        bq_scr[j, 0:TQ, :] = jnp.where(low, qq, 0.0).astype(BF16)
        bq_scr[j, TQ:2 * TQ, :] = jnp.where(low, 0.0, qq).astype(BF16)

    q_chunk = (i * TQ + lax.broadcasted_iota(I32, (1, LANES), 1)) >> CHUNK_SHIFT
    w_scaled = [wt_ref[h:h + 1, :] * idx_scale for h in range(IDX_HEADS)]
    row_iota = lax.broadcasted_iota(I32, (KEY_CHUNK, LANES), 0)

    def score_chunk(ci, carry):
        base = pl.multiple_of(ci * KEY_CHUNK, KEY_CHUNK)
        kc = kidx_ref[pl.ds(base, KEY_CHUNK), :]
        acc = jnp.zeros((KEY_CHUNK, LANES), F32)
        for j in range(IDX_HEADS // 2):
            d = lax.dot_general(kc, bidx_scr[j], DN_LAST, preferred_element_type=F32)
            acc = acc + jnp.maximum(d[:, :TQ], 0.0) * w_scaled[2 * j]
            acc = acc + jnp.maximum(d[:, TQ:], 0.0) * w_scaled[2 * j + 1]
        admissible = ((base + row_iota) >> CHUNK_SHIFT) <= q_chunk
        key = _f32_key(jnp.where(admissible, acc, -jnp.inf))
        key_scr[pl.ds(base, KEY_CHUNK), :] = key
        hi_scr[pl.ds(base, KEY_CHUNK), :] = (key >> 16).astype(I16)
        lo_scr[pl.ds(base, KEY_CHUNK), :] = (key ^ 0x8000).astype(I16)
        return carry

    lax.fori_loop(0, n_chunks, score_chunk, 0)

    def count_where(pred):
        def body(ci, a):
            base = pl.multiple_of(ci * KEY_CHUNK, KEY_CHUNK)
            hit = pred(key_scr[pl.ds(base, KEY_CHUNK), :], base).astype(I32)
            return a + jnp.sum(hit.reshape(KEY_CHUNK // 8, 8, LANES), axis=0)
        a = lax.fori_loop(0, n_chunks, body, jnp.zeros((8, LANES), I32))
        return jnp.sum(a, axis=0, keepdims=True)

    def count_half(half_ref, trial):
        t16 = jnp.broadcast_to(trial.astype(I16), (PACK, LANES))
        one, zero = jnp.ones((PACK, LANES), I16), jnp.zeros((PACK, LANES), I16)

        def body(ci, accs):
            base = pl.multiple_of(ci * KEY_CHUNK, KEY_CHUNK)
            accs = list(accs)
            chunk = half_ref[pl.ds(base, KEY_CHUNK), :]
            for r in range(KEY_CHUNK // PACK):
                hit = jnp.where(chunk[r * PACK:(r + 1) * PACK] >= t16, one, zero)
                accs[r % len(accs)] = accs[r % len(accs)] + hit
            return tuple(accs)

        accs = lax.fori_loop(0, n_chunks, body, (zero,) * 4)
        total = (accs[0] + accs[1]) + (accs[2] + accs[3])
        return jnp.sum(total.astype(I32), axis=0, keepdims=True)

    def largest_with(half_ref, need, n_all):
        def step(p, carry):
            d, n_d = carry
            trial = d + lax.shift_left(jnp.int32(1), 15 - p)
            cnt = count_half(half_ref, trial)
            ok = cnt >= need
            return jnp.where(ok, trial, d), jnp.where(ok, cnt, n_d)
        return lax.fori_loop(0, 16, step, (jnp.full((1, LANES), I16_MIN, I32), n_all))

    n_rows = jnp.zeros((1, LANES), I32) + n_chunks * KEY_CHUNK
    top, n_ge_top = largest_with(hi_scr, k_top, n_rows)
    n_gt_top = jnp.where(top == I16_MAX, 0, count_half(hi_scr, jnp.minimum(top + 1, I16_MAX)))

    top16 = jnp.broadcast_to(top.astype(I16), (KEY_CHUNK, LANES))

    def keep_top_group(ci, carry):
        rows = pl.ds(pl.multiple_of(ci * KEY_CHUNK, KEY_CHUNK), KEY_CHUNK)
        lo_scr[rows, :] = jnp.where(hi_scr[rows, :] == top16, lo_scr[rows, :], jnp.int16(I16_MIN))
        return carry

    lax.fori_loop(0, n_chunks, keep_top_group, 0)
    low, n_ge_low = largest_with(lo_scr, k_top - n_gt_top, n_ge_top - n_gt_top)
    thr = (top << 16) | (low - I16_MIN)
    cnt_thr = n_gt_top + n_ge_low

    def write_mask(select):
        def body(ci, carry):
            base = pl.multiple_of(ci * KEY_CHUNK, KEY_CHUNK)
            blk = key_scr[pl.ds(base, KEY_CHUNK), :]
            sel = select(blk, base) & (blk > KEY_NEG_INF) & (blk < KEY_POS_INF)
            mb_scr[pl.ds(base, KEY_CHUNK), :] = jnp.where(sel, 0.0, -jnp.inf)
            return carry
        lax.fori_loop(0, n_chunks, body, 0)

    write_mask(lambda blk, base: blk >= thr)

    tie = (cnt_thr > k_top) & (thr > KEY_NEG_INF) & (thr < KEY_POS_INF)

    @pl.when(jnp.max(tie.astype(I32)) > 0)
    def _break_ties():
        need = k_top - count_where(lambda blk, base: blk > thr)

        def grow(p, last):
            trial = last + lax.shift_left(jnp.int32(1), idx_bits - 1 - p)
            below = count_where(lambda blk, base: (blk == thr) & ((base + row_iota) < trial))
            return jnp.where(below <= need - 1, trial, last)

        last = lax.fori_loop(0, idx_bits, grow, jnp.zeros((1, LANES), I32))
        last = jnp.where(tie, last, INT_MAX)
        write_mask(lambda blk, base: (blk > thr) | ((blk == thr) & ((base + row_iota) <= last)))

    blocks_per_chunk = KEY_CHUNK // TQ
    win_blk = jnp.maximum(i - (blocks_per_chunk - 1), 0)
    win0 = pl.multiple_of(win_blk * TQ, TQ)
    mbw_scr[...] = mb_scr[pl.ds(win0, KEY_CHUNK), :]
    mb_scr[pl.ds(win0, KEY_CHUNK), :] = jnp.full((KEY_CHUNK, LANES), -jnp.inf, F32)
    n_far = (win_blk + blocks_per_chunk - 1) // blocks_per_chunk

    def window_mask(h):
        tiles = []
        for t in range(blocks_per_chunk):
            d = i - (win_blk + t)
            tiles.append(bias_scr[jnp.where(d == 0, 0, jnp.where(d == 1, 1, 2)), h])
        return mbw_scr[...] + jnp.concatenate(tiles, axis=0)

    m_scr[...] = jnp.full((A_HEADS, TQ), M_INIT, F32)
    acc_scr[...] = jnp.zeros((A_HEADS, ACC_ROWS, TQ), F32)
    ones = jnp.ones((ACC_ROWS - A_HEAD_DIM, KEY_CHUNK), BF16)

    def stage_a(row0, mask_of, s_buf, c_buf):
        for j in range(A_HEADS // 2):
            kc = k_ref[pl.ds(row0, KEY_CHUNK), j * LANES:(j + 1) * LANES]
            s = lax.dot_general(kc, bq_scr[j], DN_LAST, preferred_element_type=F32)
            for hh in range(2):
                h = 2 * j + hh
                sh = s[:, hh * TQ:(hh + 1) * TQ] + mask_of(h)
                s_buf[h] = sh
                c_buf[h:h + 1, :] = jnp.max(sh, axis=0, keepdims=True)

    def stage_a_far(ci, s_buf, c_buf):
        row0 = pl.multiple_of(ci * KEY_CHUNK, KEY_CHUNK)
        mb = mb_scr[pl.ds(row0, KEY_CHUNK), :]
        stage_a(row0, lambda h: mb, s_buf, c_buf)

    def stage_b(blk0, s_buf, c_buf):
        for h in range(A_HEADS):
            m_old = m_scr[h:h + 1, :]
            m_new = jnp.maximum(m_old, c_buf[h:h + 1, :])
            p = jnp.exp(s_buf[h] - m_new).astype(BF16)
            vt = [vt_ref[blk0 + t, h * A_HEAD_DIM:(h + 1) * A_HEAD_DIM, :] for t in range(blocks_per_chunk)]
            pv = jnp.dot(jnp.concatenate([jnp.concatenate(vt, axis=1), ones], axis=0), p,
                         preferred_element_type=F32)
            acc_scr[h] = jnp.exp(m_old - m_new) * acc_scr[h] + pv
            m_scr[h:h + 1, :] = m_new

    stage_a(win0, window_mask, sa_scr, ca_scr)

    def item_pair(t, carry):
        stage_a_far(2 * t, sb_scr, cb_scr)
        stage_b(jnp.where(t == 0, win_blk, (2 * t - 1) * blocks_per_chunk), sa_scr, ca_scr)

        @pl.when(2 * t + 1 < n_far)
        def _():
            stage_a_far(2 * t + 1, sa_scr, ca_scr)
            stage_b(2 * t * blocks_per_chunk, sb_scr, cb_scr)

        return carry

    lax.fori_loop(0, (n_far + 1) // 2, item_pair, 0)
    last_blk = jnp.where(n_far == 0, win_blk, (n_far - 1) * blocks_per_chunk)

    @pl.when(lax.rem(n_far, 2) == 1)
    def _():
        stage_b(last_blk, sb_scr, cb_scr)

    @pl.when(lax.rem(n_far, 2) == 0)
    def _():
        stage_b(last_blk, sa_scr, ca_scr)

    for h in range(A_HEADS):
        ot_scr[h * A_HEAD_DIM:(h + 1) * A_HEAD_DIM, :] = (
            acc_scr[h, 0:A_HEAD_DIM, :] / acc_scr[h, A_HEAD_DIM:A_HEAD_DIM + 1, :])

    out_ref[...] = ot_scr[...].T


def _t5_bucket(rel):
    half = REL_BUCKETS // 2
    max_exact = half // 2
    base = jnp.where(rel > 0, half, 0).astype(jnp.int32)
    n = jnp.abs(rel)
    nf = jnp.maximum(n, 1).astype(jnp.float32)
    large = max_exact + (jnp.log(nf / max_exact) / math.log(REL_MAX_DIST / max_exact)
                         * (half - max_exact)).astype(jnp.int32)
    large = jnp.minimum(large, half - 1)
    return base + jnp.where(n < max_exact, n, large)


def _attention(nat, wt, vt, rel_bias):
    bsz, seq, _ = nat.shape
    k_top = min(TOPK_MAX, seq // 4)
    idx_bits = max(1, int(math.ceil(math.log2(seq))))
    r = jnp.arange(TQ, dtype=jnp.int32)[:, None]
    t = jnp.arange(TQ, dtype=jnp.int32)[None, :]
    buckets = jnp.stack([_t5_bucket(r - t - TQ * d) for d in range(2)])
    return pl.pallas_call(
        functools.partial(_attn_kernel, k_top=k_top, idx_bits=idx_bits),
        grid=(bsz, seq // TQ),
        in_specs=[pl.BlockSpec(memory_space=pltpu.SMEM),
                  pl.BlockSpec((2, TQ, LANES), lambda b, i: (0, 0, 0)),
                  pl.BlockSpec((None, TQ, A_WIDTH), lambda b, i: (b, i, 0)),
                  pl.BlockSpec((None, TQ, A_WIDTH), lambda b, i: (b, i, 2)),
                  pl.BlockSpec((None, 16, TQ), lambda b, i: (b, 0, i)),
                  pl.BlockSpec((None, seq, LANES), lambda b, i: (b, 0, 3 * A_WIDTH // LANES)),
                  pl.BlockSpec((None, seq, A_WIDTH), lambda b, i: (b, 0, 1)),
                  pl.BlockSpec((None, seq // LANES, A_WIDTH, LANES), lambda b, i: (b, 0, 0, 0))],
        out_specs=pl.BlockSpec((None, TQ, A_WIDTH), lambda b, i: (b, i, 0)),
        out_shape=jax.ShapeDtypeStruct((bsz, seq, A_WIDTH), F32),
        scratch_shapes=[pltpu.VMEM((seq, LANES), I32),
                        pltpu.VMEM((seq, LANES), I16),
                        pltpu.VMEM((seq, LANES), I16),
                        pltpu.VMEM((seq, LANES), F32),
                        pltpu.VMEM((KEY_CHUNK, LANES), F32),
                        pltpu.VMEM((3, A_HEADS, TQ, LANES), F32),
                        pltpu.VMEM((A_HEADS // 2, 2 * TQ, LANES), BF16),
                        pltpu.VMEM((A_HEADS // 2, 2 * TQ, LANES), BF16),
                        pltpu.VMEM((A_HEADS, KEY_CHUNK, TQ), F32),
                        pltpu.VMEM((A_HEADS, KEY_CHUNK, TQ), F32),
                        pltpu.VMEM((A_HEADS, TQ), F32),
                        pltpu.VMEM((A_HEADS, TQ), F32),
                        pltpu.VMEM((A_HEADS, TQ), F32),
                        pltpu.VMEM((A_HEADS, ACC_ROWS, TQ), F32),
                        pltpu.VMEM((A_WIDTH, TQ), F32)],
        compiler_params=pltpu.CompilerParams(dimension_semantics=("arbitrary", "arbitrary"),
                                             vmem_limit_bytes=VMEM_LIMIT),
        name="sparse_attn",
    )(rel_bias, buckets, nat, nat, wt, nat, nat, vt)


def _out_kernel(attn_ref, za_ref, cb_ref, ga_ref, gb_ref, x_ref, p_ref, wa_ref, wb_ref, wo_ref,
                pg_ref, wpg_ref, wpp_ref, fg_ref, out_ref):
    za = za_ref[...]
    ya_in = (attn_ref[...] * (za * _sigmoid(za))).astype(BF16)
    y_a = jnp.dot(ya_in, wa_ref[...], preferred_element_type=F32)
    y_b = jnp.dot(cb_ref[...], wb_ref[...], preferred_element_type=F32)
    merged = _sigmoid(ga_ref[...]) * y_a + _sigmoid(gb_ref[...]) * y_b
    x1 = x_ref[...] + jnp.dot(merged.astype(BF16), wo_ref[...], preferred_element_type=F32)
    e = jnp.dot(p_ref[...].astype(BF16), wpp_ref[...], preferred_element_type=F32)
    gate = _sigmoid(jnp.dot(_rms(x1, pg_ref[...]).astype(BF16), wpg_ref[...], preferred_element_type=F32))
    out_ref[...] = _rms(x1 + gate * e, fg_ref[...])


def _out_stage(attn, ew, cbr, x, p, wa, wb, wo, pg, wpg, wpp, fg, tm):
    bsz, seq, d = x.shape
    full = lambda a: pl.BlockSpec(a.shape, lambda b, s: (0,) * a.ndim)
    return pl.pallas_call(
        _out_kernel,
        grid=(bsz, seq // tm),
        in_specs=[pl.BlockSpec((None, tm, A_WIDTH), lambda b, s: (b, s, 0)),
                  pl.BlockSpec((None, tm, A_WIDTH), lambda b, s: (b, s, 2)),
                  pl.BlockSpec((None, tm, B_WIDTH), lambda b, s: (b, s, 0)),
                  pl.BlockSpec((None, tm, D_MODEL), lambda b, s: (b, s, 2)),
                  pl.BlockSpec((None, tm, D_MODEL), lambda b, s: (b, s, 3)),
                  pl.BlockSpec((None, tm, d), lambda b, s: (b, s, 0)),
                  pl.BlockSpec((None, tm, PLE_DIM), lambda b, s: (b, s, 0)),
                  full(wa), full(wb), full(wo), full(pg), full(wpg), full(wpp), full(fg)],
        out_specs=pl.BlockSpec((None, tm, d), lambda b, s: (b, s, 0)),
        out_shape=jax.ShapeDtypeStruct((bsz, seq, d), F32),
        compiler_params=pltpu.CompilerParams(dimension_semantics=("arbitrary", "arbitrary"),
                                             vmem_limit_bytes=VMEM_LIMIT),
        name="merge_out",
    )(attn, ew, cbr, ew, ew, x, p, wa, wb, wo, pg, wpg, wpp, fg)


def kernel(x, p, norm_in_g, w_in, conv_w, conv_b, conv_ln_g, conv_ln_b, w_branch_a, w_branch_b, w_out,
           ple_norm_g, w_ple_gate, w_ple_proj, rel_bias, final_norm_g):
    seq = x.shape[1]
    assert w_in.shape[0] == 1, "one layer: the fused output stage ends with the final norm"
    assert seq % KEY_CHUNK == 0 and seq >= 2 * TQ and x.shape[2] == D_MODEL
    offs = np.cumsum((0,) + IN_SIZES)
    row = lambda v: v.reshape(1, -1)
    for i in range(1):
        w = w_in[i]
        wq, wk, wv, wza, wqi, wki, wwi, wglu, wzb, wga, wgb = [w[:, offs[n]:offs[n + 1]] for n in range(11)]
        wnat = jnp.concatenate([wq, wk, wqi, wki, wki], axis=1).astype(BF16)
        wew = jnp.concatenate([wglu, wza, wzb, wga, wgb], axis=1).astype(BF16)
        wvt = wv.T.astype(BF16)
        wwt = jnp.pad(wwi.T, ((0, 16 - IDX_HEADS), (0, 0))).astype(BF16)
        nat, ew, vt, wt = _proj(x, row(norm_in_g[i]), wnat, wew, wvt, wwt, tm=256)
        attn = _attention(nat, wt, vt, rel_bias)
        cbr = _conv_branch(ew, conv_w[i].reshape(CONV_K, B_WIDTH), row(conv_b[i]), row(conv_ln_g[i]),
                           row(conv_ln_b[i]), ts=512)
        x = _out_stage(attn, ew, cbr, x, p[i], w_branch_a[i].astype(BF16), w_branch_b[i].astype(BF16),
                       w_out[i].astype(BF16), row(ple_norm_g[i]), w_ple_gate[i].astype(BF16),
                       w_ple_proj[i].astype(BF16), row(final_norm_g), tm=256)
    return x
```

```python
import functools
import math

import numpy as np
import jax
import jax.numpy as jnp
from jax import lax
from jax.experimental import pallas as pl
from jax.experimental.pallas import tpu as pltpu

D_MODEL = 1024
CHUNK = 64
CHUNK_SHIFT = 6
A_HEADS = 8
A_HEAD_DIM = 64
A_WIDTH = A_HEADS * A_HEAD_DIM
IDX_HEADS = 8
IDX_DIM = 64
TOPK_MAX = 256
REL_BUCKETS = 32
REL_MAX_DIST = 128
B_WIDTH = 512
CONV_K = 31
PLE_DIM = 256
EPS = 1e-6
IN_SIZES = (A_WIDTH, A_WIDTH, A_WIDTH, A_WIDTH, IDX_HEADS * IDX_DIM, IDX_DIM, IDX_HEADS,
            2 * B_WIDTH, B_WIDTH, D_MODEL, D_MODEL)

LANES = 128
SUBLANES = 8
TQ = 128
KEY_CHUNK = 512
ACC_ROWS = A_HEAD_DIM + 16
FAR_BUCKET = 15
CONV_HALO = 32
VMEM_LIMIT = 48 * 1024 * 1024

INT_MIN = -2 ** 31
INT_MAX = 2 ** 31 - 1
KEY_NEG_INF = (0xFF800000 ^ 0x7FFFFFFF) - 2 ** 32
KEY_POS_INF = 0x7F800000
UNCHECKED_BITS = 20
M_INIT = -3.0e38

F32 = jnp.float32
BF16 = jnp.bfloat16
I32 = jnp.int32
DN_LAST = (((1,), (1,)), ((), ()))


def _rms(x, g):
    return x * lax.rsqrt(jnp.mean(x * x, axis=-1, keepdims=True) + EPS) * g


def _sigmoid(x):
    return 1.0 / (1.0 + jnp.exp(-x))


def _proj_kernel(x_ref, g_ref, wnat_ref, wglu_ref, wzb_ref, wew_ref, wvt_ref, wwt_ref, cw_ref, cb_ref, lg_ref,
                 lb_ref, nat_ref, ew_ref, vt_ref, wt_ref, cbr_ref, ext_ref, *, row_tile):
    s = pl.program_id(1)
    tm = x_ref.shape[0]
    h = _rms(x_ref[...], g_ref[...]).astype(BF16)

    @pl.when(s == 0)
    def _():
        ext_ref[0:CONV_HALO, :] = jnp.zeros((CONV_HALO, B_WIDTH), F32)
        ext_ref[CONV_HALO + tm:, :] = jnp.zeros((SUBLANES, B_WIDTH), F32)

    @pl.when(s > 0)
    def _():
        ext_ref[0:CONV_HALO, :] = ext_ref[tm:tm + CONV_HALO, :]

    glu = jnp.dot(h, wglu_ref[...], preferred_element_type=F32)
    ext_ref[CONV_HALO:CONV_HALO + tm, :] = glu[:, :B_WIDTH] * _sigmoid(glu[:, B_WIDTH:])
    zb_all = jnp.dot(h, wzb_ref[...], preferred_element_type=F32)
    first = CONV_HALO - (CONV_K - 1)
    for r in range(tm // row_tile):
        r0 = r * row_tile
        acc = jnp.zeros((row_tile, B_WIDTH), F32)
        for res in range(SUBLANES):
            part = jnp.zeros((row_tile + SUBLANES, B_WIDTH), F32)
            for off in range(res, first + CONV_K, SUBLANES):
                if off >= first:
                    part = part + (ext_ref[r0 + off - res:r0 + off - res + row_tile + SUBLANES, :]
                                   * cw_ref[off - first:off - first + 1, :])
            acc = acc + part[res:res + row_tile]
        acc = acc + cb_ref[...]
        mu = jnp.mean(acc, axis=-1, keepdims=True)
        xc = acc - mu
        var = jnp.mean(xc * xc, axis=-1, keepdims=True)
        y = xc * lax.rsqrt(var + EPS) * lg_ref[...] + lb_ref[...]
        zb = zb_all[r0:r0 + row_tile]
        cbr_ref[r0:r0 + row_tile, :] = ((y * _sigmoid(y)) * (zb * _sigmoid(zb))).astype(BF16)

    nat_ref[...] = jnp.dot(h, wnat_ref[...], preferred_element_type=F32).astype(BF16)
    ew_ref[...] = jnp.dot(h, wew_ref[...], preferred_element_type=F32)
    vt = lax.dot_general(wvt_ref[...], h, DN_LAST, preferred_element_type=F32).astype(BF16)
    for c in range(vt_ref.shape[0]):
        vt_ref[c] = vt[:, c * LANES:(c + 1) * LANES]
    wt_ref[...] = lax.dot_general(wwt_ref[...], h, DN_LAST, preferred_element_type=F32)


def _proj(x, g, wnat, wglu, wzb, wew, wvt, wwt, cw, cb, lg, lb, tm):
    bsz, seq, d = x.shape
    n_nat, n_ew = wnat.shape[1], wew.shape[1]
    full = lambda a: pl.BlockSpec(a.shape, lambda b, s: (0,) * a.ndim)
    return pl.pallas_call(
        functools.partial(_proj_kernel, row_tile=64),
        grid=(bsz, seq // tm),
        in_specs=[pl.BlockSpec((None, tm, d), lambda b, s: (b, s, 0)),
                  full(g), full(wnat), full(wglu), full(wzb), full(wew), full(wvt), full(wwt),
                  full(cw), full(cb), full(lg), full(lb)],
        out_specs=[pl.BlockSpec((None, tm, n_nat), lambda b, s: (b, s, 0)),
                   pl.BlockSpec((None, tm, n_ew), lambda b, s: (b, s, 0)),
                   pl.BlockSpec((None, tm // LANES, A_WIDTH, LANES), lambda b, s: (b, s, 0, 0)),
                   pl.BlockSpec((None, 16, tm), lambda b, s: (b, 0, s)),
                   pl.BlockSpec((None, tm, B_WIDTH), lambda b, s: (b, s, 0))],
        out_shape=[jax.ShapeDtypeStruct((bsz, seq, n_nat), BF16),
                   jax.ShapeDtypeStruct((bsz, seq, n_ew), F32),
                   jax.ShapeDtypeStruct((bsz, seq // LANES, A_WIDTH, LANES), BF16),
                   jax.ShapeDtypeStruct((bsz, 16, seq), F32),
                   jax.ShapeDtypeStruct((bsz, seq, B_WIDTH), BF16)],
        scratch_shapes=[pltpu.VMEM((tm + CONV_HALO + SUBLANES, B_WIDTH), F32)],
        compiler_params=pltpu.CompilerParams(dimension_semantics=("arbitrary", "arbitrary"),
                                             vmem_limit_bytes=VMEM_LIMIT),
        name="in_proj_conv",
    )(x, g, wnat, wglu, wzb, wew, wvt, wwt, cw, cb, lg, lb)


def _f32_key(x):
    b = lax.bitcast_convert_type(x, I32)
    return b ^ ((b >> 31) & 0x7FFFFFFF)


def _attn_kernel(relb_ref, bucket_ref, q_ref, qi_ref, wt_ref, kidx_ref, k_ref, vt_ref, out_ref,
                 key_scr, tid_scr, mb_scr, mbw_scr, bias_scr, bidx_scr, bq_scr, sa_scr, sb_scr,
                 ca_scr, cb_scr, m_scr, acc_scr, ot_scr, *, k_top, idx_bits):
    i = pl.program_id(1)
    n_chunks = i // (KEY_CHUNK // TQ) + 1
    idx_scale = (IDX_DIM ** -0.5) * (IDX_HEADS ** -0.5)
    attn_scale = A_HEAD_DIM ** -0.5

    @pl.when(i == 0)
    def _init_bias():
        for h in range(A_HEADS):
            bias_scr[2, h] = jnp.zeros((TQ, LANES), F32)
        for d in range(2):
            for h in range(A_HEADS):
                bias_scr[d, h] = jnp.zeros((TQ, LANES), F32) + relb_ref[0, h]

        def fill(b, carry):
            for d in range(2):
                hit = bucket_ref[d] == b
                for h in range(A_HEADS):
                    bias_scr[d, h] = jnp.where(hit, relb_ref[b, h], bias_scr[d, h])
            return carry

        lax.fori_loop(1, REL_BUCKETS, fill, 0)
        for d in range(2):
            for h in range(A_HEADS):
                bias_scr[d, h] = bias_scr[d, h] - relb_ref[FAR_BUCKET, h]

    lane = lax.broadcasted_iota(I32, (TQ, LANES), 1)
    low = lane < A_HEAD_DIM
    for j in range(A_HEADS // 2):
        qi = qi_ref[:, j * LANES:(j + 1) * LANES].astype(F32)
        bidx_scr[j, 0:TQ, :] = jnp.where(low, qi, 0.0).astype(BF16)
        bidx_scr[j, TQ:2 * TQ, :] = jnp.where(low, 0.0, qi).astype(BF16)
        qq = q_ref[:, j * LANES:(j + 1) * LANES].astype(F32) * attn_scale
        bq_scr[j, 0:TQ, :] = jnp.where(low, qq, 0.0).astype(BF16)
        bq_scr[j, TQ:2 * TQ, :] = jnp.where(low, 0.0, qq).astype(BF16)

    q_chunk = (i * TQ + lax.broadcasted_iota(I32, (1, LANES), 1)) >> CHUNK_SHIFT
    w_scaled = [wt_ref[h:h + 1, :] * idx_scale for h in range(IDX_HEADS)]
    row_iota = lax.broadcasted_iota(I32, (KEY_CHUNK, LANES), 0)

    def score_chunk(ci, carry):
        base = pl.multiple_of(ci * KEY_CHUNK, KEY_CHUNK)
        kc = kidx_ref[pl.ds(base, KEY_CHUNK), :]
        acc = jnp.zeros((KEY_CHUNK, LANES), F32)
        for j in range(IDX_HEADS // 2):
            d = lax.dot_general(kc, bidx_scr[j], DN_LAST, preferred_element_type=F32)
            acc = acc + jnp.maximum(d[:, :TQ], 0.0) * w_scaled[2 * j]
            acc = acc + jnp.maximum(d[:, TQ:], 0.0) * w_scaled[2 * j + 1]
        admissible = ((base + row_iota) >> CHUNK_SHIFT) <= q_chunk
        key_scr[pl.ds(base, KEY_CHUNK), :] = _f32_key(jnp.where(admissible, acc, -jnp.inf))
        return carry

    lax.fori_loop(0, n_chunks, score_chunk, 0)

    def count_where(src_ref, pred):
        def body(ci, a):
            base = pl.multiple_of(ci * KEY_CHUNK, KEY_CHUNK)
            hit = pred(src_ref[pl.ds(base, KEY_CHUNK), :]).astype(I32)
            return a + jnp.sum(hit.reshape(KEY_CHUNK // SUBLANES, SUBLANES, LANES), axis=0)
        a = lax.fori_loop(0, n_chunks, body, jnp.zeros((SUBLANES, LANES), I32))
        return jnp.sum(a, axis=0, keepdims=True)

    n_rows = jnp.zeros((1, LANES), I32) + n_chunks * KEY_CHUNK
    n_ge0 = count_where(key_scr, lambda blk: blk >= 0)
    n_gt0 = count_where(key_scr, lambda blk: blk >= 1)
    keep_all = ((q_chunk + 1) << CHUNK_SHIFT) <= k_top
    at_zero = (n_ge0 >= k_top) & (n_gt0 < k_top)
    frozen = keep_all | at_zero
    thr0 = jnp.where(keep_all, KEY_NEG_INF, jnp.where(n_ge0 >= k_top, 0, INT_MIN))
    cnt0 = jnp.where(keep_all | (n_ge0 < k_top), n_rows, n_ge0)

    def is_open(cnt_c):
        return (cnt_c != k_top) & ~frozen

    def add_bit(p, carry):
        c, cnt_c = carry
        trial = c + jnp.where(p < 32, lax.shift_left(jnp.int32(1), jnp.maximum(31 - p, 0)), 0)
        cnt = count_where(key_scr, lambda blk: blk >= trial)
        ok = (cnt >= k_top) & is_open(cnt_c)
        return jnp.where(ok, trial, c), jnp.where(ok, cnt, cnt_c)

    def two_bits(carry):
        c, cnt_c, p, _ = carry
        c, cnt_c = add_bit(p + 1, add_bit(p, (c, cnt_c)))
        return c, cnt_c, p + 2, jnp.sum(is_open(cnt_c).astype(I32))

    thr1, cnt1 = lax.fori_loop(1, UNCHECKED_BITS, add_bit, (thr0, cnt0))
    thr, cnt_thr, _, _ = lax.while_loop(
        lambda carry: (carry[2] < 32) & (carry[3] > 0), two_bits,
        (thr1, cnt1, jnp.int32(UNCHECKED_BITS), jnp.sum(is_open(cnt1).astype(I32))))

    tie = (cnt_thr > k_top) & (thr > KEY_NEG_INF) & (thr < KEY_POS_INF)
    any_tie = jnp.max(tie.astype(I32)) > 0

    def write_mask(select):
        def body(ci, carry):
            base = pl.multiple_of(ci * KEY_CHUNK, KEY_CHUNK)
            blk = key_scr[pl.ds(base, KEY_CHUNK), :]
            sel = select(blk, base) & (blk > KEY_NEG_INF) & (blk < KEY_POS_INF)
            mb_scr[pl.ds(base, KEY_CHUNK), :] = jnp.where(sel, 0.0, -jnp.inf)
            return carry
        lax.fori_loop(0, n_chunks, body, 0)

    @pl.when(jnp.logical_not(any_tie))
    def _():
        write_mask(lambda blk, base: blk >= thr)

    @pl.when(any_tie)
    def _break_ties():
        need = k_top - count_where(key_scr, lambda blk: blk > thr)

        def tied_index(ci, carry):
            base = pl.multiple_of(ci * KEY_CHUNK, KEY_CHUNK)
            tied = key_scr[pl.ds(base, KEY_CHUNK), :] == thr
            tid_scr[pl.ds(base, KEY_CHUNK), :] = jnp.where(tied, base + row_iota, INT_MAX)
            return carry

        lax.fori_loop(0, n_chunks, tied_index, 0)

        def grow(p, last):
            trial = last + lax.shift_left(jnp.int32(1), idx_bits - 1 - p)
            below = count_where(tid_scr, lambda blk: blk < trial)
            return jnp.where(below <= need - 1, trial, last)

        last = lax.fori_loop(0, idx_bits, grow, jnp.zeros((1, LANES), I32))
        last = jnp.where(tie, last, INT_MAX - 1)

        def body(ci, carry):
            base = pl.multiple_of(ci * KEY_CHUNK, KEY_CHUNK)
            blk = key_scr[pl.ds(base, KEY_CHUNK), :]
            sel = (blk > thr) | (tid_scr[pl.ds(base, KEY_CHUNK), :] <= last)
            sel = sel & (blk > KEY_NEG_INF) & (blk < KEY_POS_INF)
            mb_scr[pl.ds(base, KEY_CHUNK), :] = jnp.where(sel, 0.0, -jnp.inf)
            return carry

        lax.fori_loop(0, n_chunks, body, 0)

    blocks_per_chunk = KEY_CHUNK // TQ
    win_blk = jnp.maximum(i - (blocks_per_chunk - 1), 0)
    win0 = pl.multiple_of(win_blk * TQ, TQ)
    mbw_scr[...] = mb_scr[pl.ds(win0, KEY_CHUNK), :]
    mb_scr[pl.ds(win0, KEY_CHUNK), :] = jnp.full((KEY_CHUNK, LANES), -jnp.inf, F32)
    n_far = (win_blk + blocks_per_chunk - 1) // blocks_per_chunk

    def window_mask(h):
        tiles = []
        for t in range(blocks_per_chunk):
            d = i - (win_blk + t)
            tiles.append(bias_scr[jnp.where(d == 0, 0, jnp.where(d == 1, 1, 2)), h])
        return mbw_scr[...] + jnp.concatenate(tiles, axis=0)

    m_scr[...] = jnp.full((A_HEADS, TQ), M_INIT, F32)
    acc_scr[...] = jnp.zeros((A_HEADS, ACC_ROWS, TQ), F32)
    ones = jnp.ones((ACC_ROWS - A_HEAD_DIM, KEY_CHUNK), BF16)

    def stage_a(row0, mask_of, s_buf, c_buf):
        for j in range(A_HEADS // 2):
            kc = k_ref[pl.ds(row0, KEY_CHUNK), j * LANES:(j + 1) * LANES]
            s = lax.dot_general(kc, bq_scr[j], DN_LAST, preferred_element_type=F32)
            for hh in range(2):
                h = 2 * j + hh
                sh = s[:, hh * TQ:(hh + 1) * TQ] + mask_of(h)
                s_buf[h] = sh
                c_buf[h:h + 1, :] = jnp.max(sh, axis=0, keepdims=True)

    def stage_a_far(ci, s_buf, c_buf):
        row0 = pl.multiple_of(ci * KEY_CHUNK, KEY_CHUNK)
        mb = mb_scr[pl.ds(row0, KEY_CHUNK), :]
        stage_a(row0, lambda h: mb, s_buf, c_buf)

    def stage_b(blk0, s_buf, c_buf):
        for h in range(A_HEADS):
            m_old = m_scr[h:h + 1, :]
            m_new = jnp.maximum(m_old, c_buf[h:h + 1, :])
            p = jnp.exp(s_buf[h] - m_new).astype(BF16)
            vt = [vt_ref[blk0 + t, h * A_HEAD_DIM:(h + 1) * A_HEAD_DIM, :] for t in range(blocks_per_chunk)]
            pv = jnp.dot(jnp.concatenate([jnp.concatenate(vt, axis=1), ones], axis=0), p,
                         preferred_element_type=F32)
            acc_scr[h] = jnp.exp(m_old - m_new) * acc_scr[h] + pv
            m_scr[h:h + 1, :] = m_new

    stage_a(win0, window_mask, sa_scr, ca_scr)

    def item_pair(t, carry):
        stage_a_far(2 * t, sb_scr, cb_scr)
        stage_b(jnp.where(t == 0, win_blk, (2 * t - 1) * blocks_per_chunk), sa_scr, ca_scr)

        @pl.when(2 * t + 1 < n_far)
        def _():
            stage_a_far(2 * t + 1, sa_scr, ca_scr)
            stage_b(2 * t * blocks_per_chunk, sb_scr, cb_scr)

        return carry

    lax.fori_loop(0, (n_far + 1) // 2, item_pair, 0)
    last_blk = jnp.where(n_far == 0, win_blk, (n_far - 1) * blocks_per_chunk)

    @pl.when(lax.rem(n_far, 2) == 1)
    def _():
        stage_b(last_blk, sb_scr, cb_scr)

    @pl.when(lax.rem(n_far, 2) == 0)
    def _():
        stage_b(last_blk, sa_scr, ca_scr)

    for h in range(A_HEADS):
        ot_scr[h * A_HEAD_DIM:(h + 1) * A_HEAD_DIM, :] = (
            acc_scr[h, 0:A_HEAD_DIM, :] / acc_scr[h, A_HEAD_DIM:A_HEAD_DIM + 1, :])

    out_ref[...] = ot_scr[...].T


def _t5_bucket(rel):
    half = REL_BUCKETS // 2
    max_exact = half // 2
    base = jnp.where(rel > 0, half, 0).astype(jnp.int32)
    n = jnp.abs(rel)
    nf = jnp.maximum(n, 1).astype(jnp.float32)
    large = max_exact + (jnp.log(nf / max_exact) / math.log(REL_MAX_DIST / max_exact)
                         * (half - max_exact)).astype(jnp.int32)
    large = jnp.minimum(large, half - 1)
    return base + jnp.where(n < max_exact, n, large)


def _attention(nat, wt, vt, rel_bias):
    bsz, seq, _ = nat.shape
    k_top = min(TOPK_MAX, seq // 4)
    idx_bits = max(1, int(math.ceil(math.log2(seq))))
    r = jnp.arange(TQ, dtype=jnp.int32)[:, None]
    t = jnp.arange(TQ, dtype=jnp.int32)[None, :]
    buckets = jnp.stack([_t5_bucket(r - t - TQ * d) for d in range(2)])
    return pl.pallas_call(
        functools.partial(_attn_kernel, k_top=k_top, idx_bits=idx_bits),
        grid=(bsz, seq // TQ),
        in_specs=[pl.BlockSpec(memory_space=pltpu.SMEM),
                  pl.BlockSpec((2, TQ, LANES), lambda b, i: (0, 0, 0)),
                  pl.BlockSpec((None, TQ, A_WIDTH), lambda b, i: (b, i, 0)),
                  pl.BlockSpec((None, TQ, A_WIDTH), lambda b, i: (b, i, 2)),
                  pl.BlockSpec((None, 16, TQ), lambda b, i: (b, 0, i)),
                  pl.BlockSpec((None, seq, LANES), lambda b, i: (b, 0, 3 * A_WIDTH // LANES)),
                  pl.BlockSpec((None, seq, A_WIDTH), lambda b, i: (b, 0, 1)),
                  pl.BlockSpec((None, seq // LANES, A_WIDTH, LANES), lambda b, i: (b, 0, 0, 0))],
        out_specs=pl.BlockSpec((None, TQ, A_WIDTH), lambda b, i: (b, i, 0)),
        out_shape=jax.ShapeDtypeStruct((bsz, seq, A_WIDTH), F32),
        scratch_shapes=[pltpu.VMEM((seq, LANES), I32),
                        pltpu.VMEM((seq, LANES), I32),
                        pltpu.VMEM((seq, LANES), F32),
                        pltpu.VMEM((KEY_CHUNK, LANES), F32),
                        pltpu.VMEM((3, A_HEADS, TQ, LANES), F32),
                        pltpu.VMEM((A_HEADS // 2, 2 * TQ, LANES), BF16),
                        pltpu.VMEM((A_HEADS // 2, 2 * TQ, LANES), BF16),
                        pltpu.VMEM((A_HEADS, KEY_CHUNK, TQ), F32),
                        pltpu.VMEM((A_HEADS, KEY_CHUNK, TQ), F32),
                        pltpu.VMEM((A_HEADS, TQ), F32),
                        pltpu.VMEM((A_HEADS, TQ), F32),
                        pltpu.VMEM((A_HEADS, TQ), F32),
                        pltpu.VMEM((A_HEADS, ACC_ROWS, TQ), F32),
                        pltpu.VMEM((A_WIDTH, TQ), F32)],
        compiler_params=pltpu.CompilerParams(dimension_semantics=("arbitrary", "arbitrary"),
                                             vmem_limit_bytes=VMEM_LIMIT),
        name="sparse_attn",
    )(rel_bias, buckets, nat, nat, wt, nat, nat, vt)


def _out_kernel(attn_ref, za_ref, cb_ref, ga_ref, gb_ref, x_ref, p_ref, wa_ref, wb_ref, wo_ref,
                pg_ref, wpg_ref, wpp_ref, fg_ref, out_ref):
    za = za_ref[...]
    ya_in = (attn_ref[...] * (za * _sigmoid(za))).astype(BF16)
    y_a = jnp.dot(ya_in, wa_ref[...], preferred_element_type=F32)
    y_b = jnp.dot(cb_ref[...], wb_ref[...], preferred_element_type=F32)
    merged = _sigmoid(ga_ref[...]) * y_a + _sigmoid(gb_ref[...]) * y_b
    x1 = x_ref[...] + jnp.dot(merged.astype(BF16), wo_ref[...], preferred_element_type=F32)
    e = jnp.dot(p_ref[...].astype(BF16), wpp_ref[...], preferred_element_type=F32)
    gate = _sigmoid(jnp.dot(_rms(x1, pg_ref[...]).astype(BF16), wpg_ref[...], preferred_element_type=F32))
    out_ref[...] = _rms(x1 + gate * e, fg_ref[...])


def _out_stage(attn, ew, cbr, x, p, wa, wb, wo, pg, wpg, wpp, fg, tm):
    bsz, seq, d = x.shape
    full = lambda a: pl.BlockSpec(a.shape, lambda b, s: (0,) * a.ndim)
    return pl.pallas_call(
        _out_kernel,
        grid=(bsz, seq // tm),
        in_specs=[pl.BlockSpec((None, tm, A_WIDTH), lambda b, s: (b, s, 0)),
                  pl.BlockSpec((None, tm, A_WIDTH), lambda b, s: (b, s, 2 * D_MODEL // A_WIDTH)),
                  pl.BlockSpec((None, tm, B_WIDTH), lambda b, s: (b, s, 0)),
                  pl.BlockSpec((None, tm, D_MODEL), lambda b, s: (b, s, 0)),
                  pl.BlockSpec((None, tm, D_MODEL), lambda b, s: (b, s, 1)),
                  pl.BlockSpec((None, tm, d), lambda b, s: (b, s, 0)),
                  pl.BlockSpec((None, tm, PLE_DIM), lambda b, s: (b, s, 0)),
                  full(wa), full(wb), full(wo), full(pg), full(wpg), full(wpp), full(fg)],
        out_specs=pl.BlockSpec((None, tm, d), lambda b, s: (b, s, 0)),
        out_shape=jax.ShapeDtypeStruct((bsz, seq, d), F32),
        compiler_params=pltpu.CompilerParams(dimension_semantics=("arbitrary", "arbitrary"),
                                             vmem_limit_bytes=VMEM_LIMIT),
        name="merge_out",
    )(attn, ew, cbr, ew, ew, x, p, wa, wb, wo, pg, wpg, wpp, fg)


def kernel(x, p, norm_in_g, w_in, conv_w, conv_b, conv_ln_g, conv_ln_b, w_branch_a, w_branch_b, w_out,
           ple_norm_g, w_ple_gate, w_ple_proj, rel_bias, final_norm_g):
    seq = x.shape[1]
    assert w_in.shape[0] == 1, "one layer: the fused output stage ends with the final norm"
    assert seq % KEY_CHUNK == 0 and seq >= 2 * TQ and x.shape[2] == D_MODEL
    offs = np.cumsum((0,) + IN_SIZES)
    row = lambda v: v.reshape(1, -1)
    for i in range(1):
        w = w_in[i]
        wq, wk, wv, wza, wqi, wki, wwi, wglu, wzb, wga, wgb = [w[:, offs[n]:offs[n + 1]] for n in range(11)]
        wnat = jnp.concatenate([wq, wk, wqi, wki, wki], axis=1).astype(BF16)
        wew = jnp.concatenate([wga, wgb, wza], axis=1).astype(BF16)
        wvt = wv.T.astype(BF16)
        wwt = jnp.pad(wwi.T, ((0, 16 - IDX_HEADS), (0, 0))).astype(BF16)
        nat, ew, vt, wt, cbr = _proj(x, row(norm_in_g[i]), wnat, wglu.astype(BF16), wzb.astype(BF16), wew, wvt,
                                     wwt, conv_w[i].reshape(CONV_K, B_WIDTH), row(conv_b[i]),
                                     row(conv_ln_g[i]), row(conv_ln_b[i]), tm=256)
        attn = _attention(nat, wt, vt, rel_bias)
        x = _out_stage(attn, ew, cbr, x, p[i], w_branch_a[i].astype(BF16), w_branch_b[i].astype(BF16),
                       w_out[i].astype(BF16), row(ple_norm_g[i]), w_ple_gate[i].astype(BF16),
                       w_ple_proj[i].astype(BF16), row(final_norm_g), tm=256)
    return x
```

```python
import functools
import math

import numpy as np
import jax
import jax.numpy as jnp
from jax import lax
from jax.experimental import pallas as pl
from jax.experimental.pallas import tpu as pltpu

D_MODEL = 1024
CHUNK = 64
CHUNK_SHIFT = 6
A_HEADS = 8
A_HEAD_DIM = 64
A_WIDTH = A_HEADS * A_HEAD_DIM
IDX_HEADS = 8
IDX_DIM = 64
TOPK_MAX = 256
REL_BUCKETS = 32
REL_MAX_DIST = 128
B_WIDTH = 512
CONV_K = 31
PLE_DIM = 256
EPS = 1e-6
IN_SIZES = (A_WIDTH, A_WIDTH, A_WIDTH, A_WIDTH, IDX_HEADS * IDX_DIM, IDX_DIM, IDX_HEADS,
            2 * B_WIDTH, B_WIDTH, D_MODEL, D_MODEL)

LANES = 128
SUBLANES = 8
TQ = 128
KEY_CHUNK = 512
ACC_ROWS = A_HEAD_DIM + 16
FAR_BUCKET = 15
CONV_HALO = 32
VMEM_LIMIT = 48 * 1024 * 1024

INT_MIN = -2 ** 31
INT_MAX = 2 ** 31 - 1
KEY_NEG_INF = (0xFF800000 ^ 0x7FFFFFFF) - 2 ** 32
KEY_POS_INF = 0x7F800000
GROUP_ROWS = 32 * SUBLANES
M_INIT = -3.0e38

F32 = jnp.float32
BF16 = jnp.bfloat16
I32 = jnp.int32
U32 = jnp.uint32
DN_LAST = (((1,), (1,)), ((), ()))


def _rms(x, g):
    return x * lax.rsqrt(jnp.mean(x * x, axis=-1, keepdims=True) + EPS) * g


def _sigmoid(x):
    return 1.0 / (1.0 + jnp.exp(-x))


def _proj_kernel(x_ref, g_ref, wnat_ref, wglu_ref, wzb_ref, wew_ref, wvt_ref, wwt_ref, cw_ref, cb_ref, lg_ref,
                 lb_ref, nat_ref, ew_ref, vt_ref, wt_ref, cbr_ref, ext_ref, *, row_tile):
    s = pl.program_id(1)
    tm = x_ref.shape[0]
    h = _rms(x_ref[...], g_ref[...]).astype(BF16)

    @pl.when(s == 0)
    def _():
        ext_ref[0:CONV_HALO, :] = jnp.zeros((CONV_HALO, B_WIDTH), F32)
        ext_ref[CONV_HALO + tm:, :] = jnp.zeros((SUBLANES, B_WIDTH), F32)

    @pl.when(s > 0)
    def _():
        ext_ref[0:CONV_HALO, :] = ext_ref[tm:tm + CONV_HALO, :]

    glu = jnp.dot(h, wglu_ref[...], preferred_element_type=F32)
    ext_ref[CONV_HALO:CONV_HALO + tm, :] = glu[:, :B_WIDTH] * _sigmoid(glu[:, B_WIDTH:])
    zb_all = jnp.dot(h, wzb_ref[...], preferred_element_type=F32)
    first = CONV_HALO - (CONV_K - 1)
    for r in range(tm // row_tile):
        r0 = r * row_tile
        acc = jnp.zeros((row_tile, B_WIDTH), F32)
        for res in range(SUBLANES):
            part = jnp.zeros((row_tile + SUBLANES, B_WIDTH), F32)
            for off in range(res, first + CONV_K, SUBLANES):
                if off >= first:
                    part = part + (ext_ref[r0 + off - res:r0 + off - res + row_tile + SUBLANES, :]
                                   * cw_ref[off - first:off - first + 1, :])
            acc = acc + part[res:res + row_tile]
        acc = acc + cb_ref[...]
        mu = jnp.mean(acc, axis=-1, keepdims=True)
        xc = acc - mu
        var = jnp.mean(xc * xc, axis=-1, keepdims=True)
        y = xc * lax.rsqrt(var + EPS) * lg_ref[...] + lb_ref[...]
        zb = zb_all[r0:r0 + row_tile]
        cbr_ref[r0:r0 + row_tile, :] = ((y * _sigmoid(y)) * (zb * _sigmoid(zb))).astype(BF16)

    nat_ref[...] = jnp.dot(h, wnat_ref[...], preferred_element_type=F32).astype(BF16)
    ew_ref[...] = jnp.dot(h, wew_ref[...], preferred_element_type=F32)
    vt = lax.dot_general(wvt_ref[...], h, DN_LAST, preferred_element_type=F32).astype(BF16)
    for c in range(vt_ref.shape[0]):
        vt_ref[c] = vt[:, c * LANES:(c + 1) * LANES]
    wt_ref[...] = lax.dot_general(wwt_ref[...], h, DN_LAST, preferred_element_type=F32)


def _proj(x, g, wnat, wglu, wzb, wew, wvt, wwt, cw, cb, lg, lb, tm):
    bsz, seq, d = x.shape
    n_nat, n_ew = wnat.shape[1], wew.shape[1]
    full = lambda a: pl.BlockSpec(a.shape, lambda b, s: (0,) * a.ndim)
    return pl.pallas_call(
        functools.partial(_proj_kernel, row_tile=64),
        grid=(bsz, seq // tm),
        in_specs=[pl.BlockSpec((None, tm, d), lambda b, s: (b, s, 0)),
                  full(g), full(wnat), full(wglu), full(wzb), full(wew), full(wvt), full(wwt),
                  full(cw), full(cb), full(lg), full(lb)],
        out_specs=[pl.BlockSpec((None, tm, n_nat), lambda b, s: (b, s, 0)),
                   pl.BlockSpec((None, tm, n_ew), lambda b, s: (b, s, 0)),
                   pl.BlockSpec((None, tm // LANES, A_WIDTH, LANES), lambda b, s: (b, s, 0, 0)),
                   pl.BlockSpec((None, 16, tm), lambda b, s: (b, 0, s)),
                   pl.BlockSpec((None, tm, B_WIDTH), lambda b, s: (b, s, 0))],
        out_shape=[jax.ShapeDtypeStruct((bsz, seq, n_nat), BF16),
                   jax.ShapeDtypeStruct((bsz, seq, n_ew), F32),
                   jax.ShapeDtypeStruct((bsz, seq // LANES, A_WIDTH, LANES), BF16),
                   jax.ShapeDtypeStruct((bsz, 16, seq), F32),
                   jax.ShapeDtypeStruct((bsz, seq, B_WIDTH), BF16)],
        scratch_shapes=[pltpu.VMEM((tm + CONV_HALO + SUBLANES, B_WIDTH), F32)],
        compiler_params=pltpu.CompilerParams(dimension_semantics=("arbitrary", "arbitrary"),
                                             vmem_limit_bytes=VMEM_LIMIT),
        name="in_proj_conv",
    )(x, g, wnat, wglu, wzb, wew, wvt, wwt, cw, cb, lg, lb)


def _f32_key(x):
    b = lax.bitcast_convert_type(x, I32)
    return b ^ ((b >> 31) & 0x7FFFFFFF)


def _bit_planes(words):
    a = list(words)
    j, m = 16, 0x0000FFFF
    while j:
        k = 0
        while k < 32:
            t = (a[k] ^ (a[k + j] >> np.uint32(j))) & np.uint32(m)
            a[k] = a[k] ^ t
            a[k + j] = a[k + j] ^ (t << np.uint32(j))
            k = (k + j + 1) & ~j
        j >>= 1
        m ^= (m << j) & 0xFFFFFFFF
    return a


def _attn_kernel(relb_ref, bucket_ref, q_ref, qi_ref, wt_ref, kidx_ref, k_ref, vt_ref, out_ref,
                 key_scr, plane_scr, tid_scr, mb_scr, mbw_scr, bias_scr, bidx_scr, bq_scr, sa_scr, sb_scr,
                 ca_scr, cb_scr, m_scr, acc_scr, ot_scr, *, k_top, idx_bits):
    i = pl.program_id(1)
    n_chunks = i // (KEY_CHUNK // TQ) + 1
    idx_scale = (IDX_DIM ** -0.5) * (IDX_HEADS ** -0.5)
    attn_scale = A_HEAD_DIM ** -0.5

    @pl.when(i == 0)
    def _init_bias():
        plane_scr[...] = jnp.zeros(plane_scr.shape, U32)
        for h in range(A_HEADS):
            bias_scr[2, h] = jnp.zeros((TQ, LANES), F32)
        for d in range(2):
            for h in range(A_HEADS):
                bias_scr[d, h] = jnp.zeros((TQ, LANES), F32) + relb_ref[0, h]

        def fill(b, carry):
            for d in range(2):
                hit = bucket_ref[d] == b
                for h in range(A_HEADS):
                    bias_scr[d, h] = jnp.where(hit, relb_ref[b, h], bias_scr[d, h])
            return carry

        lax.fori_loop(1, REL_BUCKETS, fill, 0)
        for d in range(2):
            for h in range(A_HEADS):
                bias_scr[d, h] = bias_scr[d, h] - relb_ref[FAR_BUCKET, h]

    lane = lax.broadcasted_iota(I32, (TQ, LANES), 1)
    low = lane < A_HEAD_DIM
    for j in range(A_HEADS // 2):
        qi = qi_ref[:, j * LANES:(j + 1) * LANES].astype(F32)
        bidx_scr[j, 0:TQ, :] = jnp.where(low, qi, 0.0).astype(BF16)
        bidx_scr[j, TQ:2 * TQ, :] = jnp.where(low, 0.0, qi).astype(BF16)
        qq = q_ref[:, j * LANES:(j + 1) * LANES].astype(F32) * attn_scale
        bq_scr[j, 0:TQ, :] = jnp.where(low, qq, 0.0).astype(BF16)
        bq_scr[j, TQ:2 * TQ, :] = jnp.where(low, 0.0, qq).astype(BF16)

    q_chunk = (i * TQ + lax.broadcasted_iota(I32, (1, LANES), 1)) >> CHUNK_SHIFT
    w_scaled = [wt_ref[h:h + 1, :] * idx_scale for h in range(IDX_HEADS)]
    row_iota = lax.broadcasted_iota(I32, (KEY_CHUNK, LANES), 0)

    def score_chunk(ci):
        base = pl.multiple_of(ci * KEY_CHUNK, KEY_CHUNK)
        kc = kidx_ref[pl.ds(base, KEY_CHUNK), :]
        acc = jnp.zeros((KEY_CHUNK, LANES), F32)
        for j in range(IDX_HEADS // 2):
            d = lax.dot_general(kc, bidx_scr[j], DN_LAST, preferred_element_type=F32)
            acc = acc + jnp.maximum(d[:, :TQ], 0.0) * w_scaled[2 * j]
            acc = acc + jnp.maximum(d[:, TQ:], 0.0) * w_scaled[2 * j + 1]
        admissible = ((base + row_iota) >> CHUNK_SHIFT) <= q_chunk
        key = _f32_key(jnp.where(admissible, acc, -jnp.inf))
        key_scr[pl.ds(base, KEY_CHUNK), :] = key
        ukey = lax.bitcast_convert_type(key, U32) ^ np.uint32(0x80000000)
        for gg in range(KEY_CHUNK // GROUP_ROWS):
            words = [ukey[gg * GROUP_ROWS + w * SUBLANES:gg * GROUP_ROWS + (w + 1) * SUBLANES] for w in range(32)]
            grp = pl.ds(pl.multiple_of(ci * (KEY_CHUNK // 32) + gg * SUBLANES, SUBLANES), SUBLANES)
            for b, plane in enumerate(_bit_planes(words)):
                plane_scr[b, grp, :] = plane

    def score_pair(t, carry):
        score_chunk(2 * t)
        score_chunk(2 * t + 1)
        return carry

    lax.fori_loop(0, n_chunks // 2, score_pair, 0)

    @pl.when(lax.rem(n_chunks, 2) == 1)
    def _():
        score_chunk(n_chunks - 1)

    def count_where(src_ref, pred):
        def body(ci, a):
            base = pl.multiple_of(ci * KEY_CHUNK, KEY_CHUNK)
            hit = pred(src_ref[pl.ds(base, KEY_CHUNK), :]).astype(I32)
            return a + jnp.sum(hit.reshape(KEY_CHUNK // SUBLANES, SUBLANES, LANES), axis=0)
        a = lax.fori_loop(0, n_chunks, body, jnp.zeros((SUBLANES, LANES), I32))
        return jnp.sum(a, axis=0, keepdims=True)

    n_groups = n_chunks * (KEY_CHUNK // GROUP_ROWS)
    max_groups = plane_scr.shape[1] // SUBLANES

    def lane_total(words):
        pc = [lax.population_count(w) for w in words]
        while len(pc) > 1:
            pc = [pc[k] + pc[k + 1] for k in range(0, len(pc) - 1, 2)] + ([pc[-1]] if len(pc) % 2 else [])
        return jnp.sum(lax.bitcast_convert_type(pc[0], I32), axis=0, keepdims=True)

    def choose_bit(b, carry):
        alive, need, prefix = carry
        planes = plane_scr[b]
        ones = [alive[g] & planes[g * SUBLANES:(g + 1) * SUBLANES] for g in range(max_groups)]
        cnt = lane_total(ones)
        take = cnt >= need
        alive = tuple(jnp.where(take, ones[g], alive[g] ^ ones[g]) for g in range(max_groups))
        bit = lax.shift_left(jnp.uint32(1), (31 - b).astype(U32))
        return alive, jnp.where(take, need, need - cnt), jnp.where(take, prefix | bit, prefix)

    alive0 = tuple(jnp.zeros((SUBLANES, LANES), U32) + jnp.where(g < n_groups, np.uint32(0xFFFFFFFF), np.uint32(0))
                   for g in range(max_groups))
    alive, need_left, prefix = lax.fori_loop(
        0, 32, choose_bit, (alive0, jnp.full((1, LANES), k_top, I32), jnp.zeros((1, LANES), U32)))
    thr = lax.bitcast_convert_type(prefix ^ np.uint32(0x80000000), I32)
    n_tied = lane_total(list(alive))

    tie = (n_tied > need_left) & (thr > KEY_NEG_INF) & (thr < KEY_POS_INF)
    any_tie = jnp.max(tie.astype(I32)) > 0

    def write_mask(select):
        def body(ci, carry):
            base = pl.multiple_of(ci * KEY_CHUNK, KEY_CHUNK)
            blk = key_scr[pl.ds(base, KEY_CHUNK), :]
            sel = select(blk, base) & (blk > KEY_NEG_INF) & (blk < KEY_POS_INF)
            mb_scr[pl.ds(base, KEY_CHUNK), :] = jnp.where(sel, 0.0, -jnp.inf)
            return carry
        lax.fori_loop(0, n_chunks, body, 0)

    @pl.when(jnp.logical_not(any_tie))
    def _():
        write_mask(lambda blk, base: blk >= thr)

    @pl.when(any_tie)
    def _break_ties():
        need = need_left

        def tied_index(ci, carry):
            base = pl.multiple_of(ci * KEY_CHUNK, KEY_CHUNK)
            tied = key_scr[pl.ds(base, KEY_CHUNK), :] == thr
            tid_scr[pl.ds(base, KEY_CHUNK), :] = jnp.where(tied, base + row_iota, INT_MAX)
            return carry

        lax.fori_loop(0, n_chunks, tied_index, 0)

        def grow(p, last):
            trial = last + lax.shift_left(jnp.int32(1), idx_bits - 1 - p)
            below = count_where(tid_scr, lambda blk: blk < trial)
            return jnp.where(below <= need - 1, trial, last)

        last = lax.fori_loop(0, idx_bits, grow, jnp.zeros((1, LANES), I32))
        last = jnp.where(tie, last, INT_MAX - 1)

        def body(ci, carry):
            base = pl.multiple_of(ci * KEY_CHUNK, KEY_CHUNK)
            blk = key_scr[pl.ds(base, KEY_CHUNK), :]
            sel = (blk > thr) | (tid_scr[pl.ds(base, KEY_CHUNK), :] <= last)
            sel = sel & (blk > KEY_NEG_INF) & (blk < KEY_POS_INF)
            mb_scr[pl.ds(base, KEY_CHUNK), :] = jnp.where(sel, 0.0, -jnp.inf)
            return carry

        lax.fori_loop(0, n_chunks, body, 0)

    blocks_per_chunk = KEY_CHUNK // TQ
    win_blk = jnp.maximum(i - (blocks_per_chunk - 1), 0)
    win0 = pl.multiple_of(win_blk * TQ, TQ)
    mbw_scr[...] = mb_scr[pl.ds(win0, KEY_CHUNK), :]
    mb_scr[pl.ds(win0, KEY_CHUNK), :] = jnp.full((KEY_CHUNK, LANES), -jnp.inf, F32)
    n_far = (win_blk + blocks_per_chunk - 1) // blocks_per_chunk

    def window_mask(h):
        tiles = []
        for t in range(blocks_per_chunk):
            d = i - (win_blk + t)
            tiles.append(bias_scr[jnp.where(d == 0, 0, jnp.where(d == 1, 1, 2)), h])
        return mbw_scr[...] + jnp.concatenate(tiles, axis=0)

    m_scr[...] = jnp.full((A_HEADS, TQ), M_INIT, F32)
    acc_scr[...] = jnp.zeros((A_HEADS, ACC_ROWS, TQ), F32)
    ones = jnp.ones((ACC_ROWS - A_HEAD_DIM, KEY_CHUNK), BF16)

    def stage_a(row0, mask_of, s_buf, c_buf):
        for j in range(A_HEADS // 2):
            kc = k_ref[pl.ds(row0, KEY_CHUNK), j * LANES:(j + 1) * LANES]
            s = lax.dot_general(kc, bq_scr[j], DN_LAST, preferred_element_type=F32)
            for hh in range(2):
                h = 2 * j + hh
                sh = s[:, hh * TQ:(hh + 1) * TQ] + mask_of(h)
                s_buf[h] = sh
                c_buf[h:h + 1, :] = jnp.max(sh, axis=0, keepdims=True)

    def stage_a_far(ci, s_buf, c_buf):
        row0 = pl.multiple_of(ci * KEY_CHUNK, KEY_CHUNK)
        mb = mb_scr[pl.ds(row0, KEY_CHUNK), :]
        stage_a(row0, lambda h: mb, s_buf, c_buf)

    def stage_b(blk0, s_buf, c_buf):
        for h in range(A_HEADS):
            m_old = m_scr[h:h + 1, :]
            m_new = jnp.maximum(m_old, c_buf[h:h + 1, :])
            p = jnp.exp(s_buf[h] - m_new).astype(BF16)
            vt = [vt_ref[blk0 + t, h * A_HEAD_DIM:(h + 1) * A_HEAD_DIM, :] for t in range(blocks_per_chunk)]
            pv = jnp.dot(jnp.concatenate([jnp.concatenate(vt, axis=1), ones], axis=0), p,
                         preferred_element_type=F32)
            acc_scr[h] = jnp.exp(m_old - m_new) * acc_scr[h] + pv
            m_scr[h:h + 1, :] = m_new

    stage_a(win0, window_mask, sa_scr, ca_scr)

    def item_pair(t, carry):
        stage_a_far(2 * t, sb_scr, cb_scr)
        stage_b(jnp.where(t == 0, win_blk, (2 * t - 1) * blocks_per_chunk), sa_scr, ca_scr)

        @pl.when(2 * t + 1 < n_far)
        def _():
            stage_a_far(2 * t + 1, sa_scr, ca_scr)
            stage_b(2 * t * blocks_per_chunk, sb_scr, cb_scr)

        return carry

    lax.fori_loop(0, (n_far + 1) // 2, item_pair, 0)
    last_blk = jnp.where(n_far == 0, win_blk, (n_far - 1) * blocks_per_chunk)

    @pl.when(lax.rem(n_far, 2) == 1)
    def _():
        stage_b(last_blk, sb_scr, cb_scr)

    @pl.when(lax.rem(n_far, 2) == 0)
    def _():
        stage_b(last_blk, sa_scr, ca_scr)

    for h in range(A_HEADS):
        ot_scr[h * A_HEAD_DIM:(h + 1) * A_HEAD_DIM, :] = (
            acc_scr[h, 0:A_HEAD_DIM, :] / acc_scr[h, A_HEAD_DIM:A_HEAD_DIM + 1, :])

    out_ref[...] = ot_scr[...].T


def _t5_bucket(rel):
    half = REL_BUCKETS // 2
    max_exact = half // 2
    base = jnp.where(rel > 0, half, 0).astype(jnp.int32)
    n = jnp.abs(rel)
    nf = jnp.maximum(n, 1).astype(jnp.float32)
    large = max_exact + (jnp.log(nf / max_exact) / math.log(REL_MAX_DIST / max_exact)
                         * (half - max_exact)).astype(jnp.int32)
    large = jnp.minimum(large, half - 1)
    return base + jnp.where(n < max_exact, n, large)


def _attention(nat, wt, vt, rel_bias):
    bsz, seq, _ = nat.shape
    k_top = min(TOPK_MAX, seq // 4)
    idx_bits = max(1, int(math.ceil(math.log2(seq))))
    r = jnp.arange(TQ, dtype=jnp.int32)[:, None]
    t = jnp.arange(TQ, dtype=jnp.int32)[None, :]
    buckets = jnp.stack([_t5_bucket(r - t - TQ * d) for d in range(2)])
    return pl.pallas_call(
        functools.partial(_attn_kernel, k_top=k_top, idx_bits=idx_bits),
        grid=(bsz, seq // TQ),
        in_specs=[pl.BlockSpec(memory_space=pltpu.SMEM),
                  pl.BlockSpec((2, TQ, LANES), lambda b, i: (0, 0, 0)),
                  pl.BlockSpec((None, TQ, A_WIDTH), lambda b, i: (b, i, 0)),
                  pl.BlockSpec((None, TQ, A_WIDTH), lambda b, i: (b, i, 2)),
                  pl.BlockSpec((None, 16, TQ), lambda b, i: (b, 0, i)),
                  pl.BlockSpec((None, seq, LANES), lambda b, i: (b, 0, 3 * A_WIDTH // LANES)),
                  pl.BlockSpec((None, seq, A_WIDTH), lambda b, i: (b, 0, 1)),
                  pl.BlockSpec((None, seq // LANES, A_WIDTH, LANES), lambda b, i: (b, 0, 0, 0))],
        out_specs=pl.BlockSpec((None, TQ, A_WIDTH), lambda b, i: (b, i, 0)),
        out_shape=jax.ShapeDtypeStruct((bsz, seq, A_WIDTH), F32),
        scratch_shapes=[pltpu.VMEM((seq, LANES), I32),
                        pltpu.VMEM((32, seq // 32, LANES), U32),
                        pltpu.VMEM((seq, LANES), I32),
                        pltpu.VMEM((seq, LANES), F32),
                        pltpu.VMEM((KEY_CHUNK, LANES), F32),
                        pltpu.VMEM((3, A_HEADS, TQ, LANES), F32),
                        pltpu.VMEM((A_HEADS // 2, 2 * TQ, LANES), BF16),
                        pltpu.VMEM((A_HEADS // 2, 2 * TQ, LANES), BF16),
                        pltpu.VMEM((A_HEADS, KEY_CHUNK, TQ), F32),
                        pltpu.VMEM((A_HEADS, KEY_CHUNK, TQ), F32),
                        pltpu.VMEM((A_HEADS, TQ), F32),
                        pltpu.VMEM((A_HEADS, TQ), F32),
                        pltpu.VMEM((A_HEADS, TQ), F32),
                        pltpu.VMEM((A_HEADS, ACC_ROWS, TQ), F32),
                        pltpu.VMEM((A_WIDTH, TQ), F32)],
        compiler_params=pltpu.CompilerParams(dimension_semantics=("arbitrary", "arbitrary"),
                                             vmem_limit_bytes=VMEM_LIMIT),
        name="sparse_attn",
    )(rel_bias, buckets, nat, nat, wt, nat, nat, vt)


def _out_kernel(attn_ref, za_ref, cb_ref, ga_ref, gb_ref, x_ref, p_ref, wa_ref, wb_ref, wo_ref,
                pg_ref, wpg_ref, wpp_ref, fg_ref, out_ref):
    za = za_ref[...]
    ya_in = (attn_ref[...] * (za * _sigmoid(za))).astype(BF16)
    y_a = jnp.dot(ya_in, wa_ref[...], preferred_element_type=F32)
    y_b = jnp.dot(cb_ref[...], wb_ref[...], preferred_element_type=F32)
    merged = _sigmoid(ga_ref[...]) * y_a + _sigmoid(gb_ref[...]) * y_b
    x1 = x_ref[...] + jnp.dot(merged.astype(BF16), wo_ref[...], preferred_element_type=F32)
    e = jnp.dot(p_ref[...].astype(BF16), wpp_ref[...], preferred_element_type=F32)
    gate = _sigmoid(jnp.dot(_rms(x1, pg_ref[...]).astype(BF16), wpg_ref[...], preferred_element_type=F32))
    out_ref[...] = _rms(x1 + gate * e, fg_ref[...])


def _out_stage(attn, ew, cbr, x, p, wa, wb, wo, pg, wpg, wpp, fg, tm):
    bsz, seq, d = x.shape
    full = lambda a: pl.BlockSpec(a.shape, lambda b, s: (0,) * a.ndim)
    return pl.pallas_call(
        _out_kernel,
        grid=(bsz, seq // tm),
        in_specs=[pl.BlockSpec((None, tm, A_WIDTH), lambda b, s: (b, s, 0)),
                  pl.BlockSpec((None, tm, A_WIDTH), lambda b, s: (b, s, 2 * D_MODEL // A_WIDTH)),
                  pl.BlockSpec((None, tm, B_WIDTH), lambda b, s: (b, s, 0)),
                  pl.BlockSpec((None, tm, D_MODEL), lambda b, s: (b, s, 0)),
                  pl.BlockSpec((None, tm, D_MODEL), lambda b, s: (b, s, 1)),
                  pl.BlockSpec((None, tm, d), lambda b, s: (b, s, 0)),
                  pl.BlockSpec((None, tm, PLE_DIM), lambda b, s: (b, s, 0)),
                  full(wa), full(wb), full(wo), full(pg), full(wpg), full(wpp), full(fg)],
        out_specs=pl.BlockSpec((None, tm, d), lambda b, s: (b, s, 0)),
        out_shape=jax.ShapeDtypeStruct((bsz, seq, d), F32),
        compiler_params=pltpu.CompilerParams(dimension_semantics=("arbitrary", "arbitrary"),
                                             vmem_limit_bytes=VMEM_LIMIT),
        name="merge_out",
    )(attn, ew, cbr, ew, ew, x, p, wa, wb, wo, pg, wpg, wpp, fg)


def kernel(x, p, norm_in_g, w_in, conv_w, conv_b, conv_ln_g, conv_ln_b, w_branch_a, w_branch_b, w_out,
           ple_norm_g, w_ple_gate, w_ple_proj, rel_bias, final_norm_g):
    seq = x.shape[1]
    assert w_in.shape[0] == 1, "one layer: the fused output stage ends with the final norm"
    assert seq % KEY_CHUNK == 0 and seq >= 2 * TQ and x.shape[2] == D_MODEL
    offs = np.cumsum((0,) + IN_SIZES)
    row = lambda v: v.reshape(1, -1)
    for i in range(1):
        w = w_in[i]
        wq, wk, wv, wza, wqi, wki, wwi, wglu, wzb, wga, wgb = [w[:, offs[n]:offs[n + 1]] for n in range(11)]
        wnat = jnp.concatenate([wq, wk, wqi, wki, wki], axis=1).astype(BF16)
        wew = jnp.concatenate([wga, wgb, wza], axis=1).astype(BF16)
        wvt = wv.T.astype(BF16)
        wwt = jnp.pad(wwi.T, ((0, 16 - IDX_HEADS), (0, 0))).astype(BF16)
        nat, ew, vt, wt, cbr = _proj(x, row(norm_in_g[i]), wnat, wglu.astype(BF16), wzb.astype(BF16), wew, wvt,
                                     wwt, conv_w[i].reshape(CONV_K, B_WIDTH), row(conv_b[i]),
                                     row(conv_ln_g[i]), row(conv_ln_b[i]), tm=256)
        attn = _attention(nat, wt, vt, rel_bias)
        x = _out_stage(attn, ew, cbr, x, p[i], w_branch_a[i].astype(BF16), w_branch_b[i].astype(BF16),
                       w_out[i].astype(BF16), row(ple_norm_g[i]), w_ple_gate[i].astype(BF16),
                       w_ple_proj[i].astype(BF16), row(final_norm_g), tm=256)
    return x
```

```python
import functools
import math

import numpy as np
import jax
import jax.numpy as jnp
from jax import lax
from jax.experimental import pallas as pl
from jax.experimental.pallas import tpu as pltpu

D_MODEL = 1024
CHUNK = 64
CHUNK_SHIFT = 6
A_HEADS = 8
A_HEAD_DIM = 64
A_WIDTH = A_HEADS * A_HEAD_DIM
IDX_HEADS = 8
IDX_DIM = 64
TOPK_MAX = 256
REL_BUCKETS = 32
REL_MAX_DIST = 128
B_WIDTH = 512
CONV_K = 31
PLE_DIM = 256
EPS = 1e-6
IN_SIZES = (A_WIDTH, A_WIDTH, A_WIDTH, A_WIDTH, IDX_HEADS * IDX_DIM, IDX_DIM, IDX_HEADS,
            2 * B_WIDTH, B_WIDTH, D_MODEL, D_MODEL)

LANES = 128
SUBLANES = 8
TQ = 128
KEY_CHUNK = 512
ACC_ROWS = A_HEAD_DIM + 16
FAR_BUCKET = 15
CONV_HALO = 32
VMEM_LIMIT = 48 * 1024 * 1024

INT_MIN = -2 ** 31
INT_MAX = 2 ** 31 - 1
KEY_NEG_INF = (0xFF800000 ^ 0x7FFFFFFF) - 2 ** 32
KEY_POS_INF = 0x7F800000
GROUP_ROWS = 32 * SUBLANES
M_INIT = -3.0e38
MASKED = -1.0e30
LOG2E = math.log2(math.e)

F32 = jnp.float32
BF16 = jnp.bfloat16
I32 = jnp.int32
U32 = jnp.uint32
DN_LAST = (((1,), (1,)), ((), ()))


def _rms(x, g):
    return x * lax.rsqrt(jnp.mean(x * x, axis=-1, keepdims=True) + EPS) * g


def _sigmoid(x):
    return 1.0 / (1.0 + jnp.exp(-x))


def _proj_kernel(x_ref, g_ref, wnat_ref, wglu_ref, wzb_ref, wew_ref, wvt_ref, wwt_ref, cw_ref, cb_ref, lg_ref,
                 lb_ref, nat_ref, ew_ref, vt_ref, wt_ref, cbr_ref, ext_ref, *, row_tile):
    s = pl.program_id(1)
    tm = x_ref.shape[0]
    h = _rms(x_ref[...], g_ref[...]).astype(BF16)

    @pl.when(s == 0)
    def _():
        ext_ref[0:CONV_HALO, :] = jnp.zeros((CONV_HALO, B_WIDTH), F32)
        ext_ref[CONV_HALO + tm:, :] = jnp.zeros((SUBLANES, B_WIDTH), F32)

    @pl.when(s > 0)
    def _():
        ext_ref[0:CONV_HALO, :] = ext_ref[tm:tm + CONV_HALO, :]

    glu = jnp.dot(h, wglu_ref[...], preferred_element_type=F32)
    ext_ref[CONV_HALO:CONV_HALO + tm, :] = glu[:, :B_WIDTH] * _sigmoid(glu[:, B_WIDTH:])
    zb_all = jnp.dot(h, wzb_ref[...], preferred_element_type=F32)
    first = CONV_HALO - (CONV_K - 1)
    for r in range(tm // row_tile):
        r0 = r * row_tile
        acc = jnp.zeros((row_tile, B_WIDTH), F32)
        for res in range(SUBLANES):
            part = jnp.zeros((row_tile + SUBLANES, B_WIDTH), F32)
            for off in range(res, first + CONV_K, SUBLANES):
                if off >= first:
                    part = part + (ext_ref[r0 + off - res:r0 + off - res + row_tile + SUBLANES, :]
                                   * cw_ref[off - first:off - first + 1, :])
            acc = acc + part[res:res + row_tile]
        acc = acc + cb_ref[...]
        mu = jnp.mean(acc, axis=-1, keepdims=True)
        xc = acc - mu
        var = jnp.mean(xc * xc, axis=-1, keepdims=True)
        y = xc * lax.rsqrt(var + EPS) * lg_ref[...] + lb_ref[...]
        zb = zb_all[r0:r0 + row_tile]
        cbr_ref[r0:r0 + row_tile, :] = ((y * _sigmoid(y)) * (zb * _sigmoid(zb))).astype(BF16)

    nat_ref[...] = jnp.dot(h, wnat_ref[...], preferred_element_type=F32).astype(BF16)
    ew_ref[...] = jnp.dot(h, wew_ref[...], preferred_element_type=F32)
    vt = lax.dot_general(wvt_ref[...], h, DN_LAST, preferred_element_type=F32).astype(BF16)
    for c in range(vt_ref.shape[0]):
        vt_ref[c] = vt[:, c * LANES:(c + 1) * LANES]
    wt_ref[...] = lax.dot_general(wwt_ref[...], h, DN_LAST, preferred_element_type=F32)


def _proj(x, g, wnat, wglu, wzb, wew, wvt, wwt, cw, cb, lg, lb, tm):
    bsz, seq, d = x.shape
    n_nat, n_ew = wnat.shape[1], wew.shape[1]
    full = lambda a: pl.BlockSpec(a.shape, lambda b, s: (0,) * a.ndim)
    return pl.pallas_call(
        functools.partial(_proj_kernel, row_tile=64),
        grid=(bsz, seq // tm),
        in_specs=[pl.BlockSpec((None, tm, d), lambda b, s: (b, s, 0)),
                  full(g), full(wnat), full(wglu), full(wzb), full(wew), full(wvt), full(wwt),
                  full(cw), full(cb), full(lg), full(lb)],
        out_specs=[pl.BlockSpec((None, tm, n_nat), lambda b, s: (b, s, 0)),
                   pl.BlockSpec((None, tm, n_ew), lambda b, s: (b, s, 0)),
                   pl.BlockSpec((None, tm // LANES, A_WIDTH, LANES), lambda b, s: (b, s, 0, 0)),
                   pl.BlockSpec((None, 16, tm), lambda b, s: (b, 0, s)),
                   pl.BlockSpec((None, tm, B_WIDTH), lambda b, s: (b, s, 0))],
        out_shape=[jax.ShapeDtypeStruct((bsz, seq, n_nat), BF16),
                   jax.ShapeDtypeStruct((bsz, seq, n_ew), F32),
                   jax.ShapeDtypeStruct((bsz, seq // LANES, A_WIDTH, LANES), BF16),
                   jax.ShapeDtypeStruct((bsz, 16, seq), F32),
                   jax.ShapeDtypeStruct((bsz, seq, B_WIDTH), BF16)],
        scratch_shapes=[pltpu.VMEM((tm + CONV_HALO + SUBLANES, B_WIDTH), F32)],
        compiler_params=pltpu.CompilerParams(dimension_semantics=("arbitrary", "arbitrary"),
                                             vmem_limit_bytes=VMEM_LIMIT),
        name="in_proj_conv",
    )(x, g, wnat, wglu, wzb, wew, wvt, wwt, cw, cb, lg, lb)


def _f32_key(x):
    b = lax.bitcast_convert_type(x, I32)
    return b ^ ((b >> 31) & 0x7FFFFFFF)


def _bit_planes(words):
    a = list(words)
    j, m = 16, 0x0000FFFF
    while j:
        k = 0
        while k < 32:
            t = (a[k] ^ (a[k + j] >> np.uint32(j))) & np.uint32(m)
            a[k] = a[k] ^ t
            a[k + j] = a[k + j] ^ (t << np.uint32(j))
            k = (k + j + 1) & ~j
        j >>= 1
        m ^= (m << j) & 0xFFFFFFFF
    return a


def _attn_kernel(relb_ref, bucket_ref, q_ref, qi_ref, wt_ref, kidx_ref, k_ref, vt_ref, out_ref,
                 key_scr, plane_scr, tid_scr, mb_scr, mbw_scr, bias_scr, bidx_scr, bq_scr, sa_scr, sb_scr,
                 ca_scr, cb_scr, m_scr, acc_scr, ot_scr, *, k_top, idx_bits):
    i = pl.program_id(1)
    n_chunks = i // (KEY_CHUNK // TQ) + 1
    idx_scale = (IDX_DIM ** -0.5) * (IDX_HEADS ** -0.5)
    attn_scale = A_HEAD_DIM ** -0.5

    @pl.when(i == 0)
    def _init_bias():
        plane_scr[...] = jnp.zeros(plane_scr.shape, U32)
        for h in range(A_HEADS):
            bias_scr[2, h] = jnp.zeros((TQ, LANES), F32)
        for d in range(2):
            for h in range(A_HEADS):
                bias_scr[d, h] = jnp.zeros((TQ, LANES), F32) + relb_ref[0, h]

        def fill(b, carry):
            for d in range(2):
                hit = bucket_ref[d] == b
                for h in range(A_HEADS):
                    bias_scr[d, h] = jnp.where(hit, relb_ref[b, h], bias_scr[d, h])
            return carry

        lax.fori_loop(1, REL_BUCKETS, fill, 0)
        for d in range(2):
            for h in range(A_HEADS):
                bias_scr[d, h] = (bias_scr[d, h] - relb_ref[FAR_BUCKET, h]) * LOG2E
        eye = (lax.broadcasted_iota(I32, (2 * TQ, LANES), 0) & (TQ - 1)) == lax.broadcasted_iota(
            I32, (2 * TQ, LANES), 1)
        for j in range(A_HEADS // 2):
            bq_scr[j, :, LANES:] = jnp.where(eye, 1.0, 0.0).astype(BF16)

    lane = lax.broadcasted_iota(I32, (TQ, LANES), 1)
    low = lane < A_HEAD_DIM
    for j in range(A_HEADS // 2):
        qi = qi_ref[:, j * LANES:(j + 1) * LANES].astype(F32)
        bidx_scr[j, 0:TQ, :] = jnp.where(low, qi, 0.0).astype(BF16)
        bidx_scr[j, TQ:2 * TQ, :] = jnp.where(low, 0.0, qi).astype(BF16)
        qq = q_ref[:, j * LANES:(j + 1) * LANES].astype(F32) * (attn_scale * LOG2E)
        bq_scr[j, 0:TQ, 0:LANES] = jnp.where(low, qq, 0.0).astype(BF16)
        bq_scr[j, TQ:2 * TQ, 0:LANES] = jnp.where(low, 0.0, qq).astype(BF16)

    q_chunk = (i * TQ + lax.broadcasted_iota(I32, (1, LANES), 1)) >> CHUNK_SHIFT
    w_scaled = [wt_ref[h:h + 1, :] * idx_scale for h in range(IDX_HEADS)]
    row_iota = lax.broadcasted_iota(I32, (KEY_CHUNK, LANES), 0)

    def score_chunk(ci):
        base = pl.multiple_of(ci * KEY_CHUNK, KEY_CHUNK)
        kc = kidx_ref[pl.ds(base, KEY_CHUNK), :]
        acc = jnp.zeros((KEY_CHUNK, LANES), F32)
        for j in range(IDX_HEADS // 2):
            d = lax.dot_general(kc, bidx_scr[j], DN_LAST, preferred_element_type=F32)
            acc = acc + jnp.maximum(d[:, :TQ], 0.0) * w_scaled[2 * j]
            acc = acc + jnp.maximum(d[:, TQ:], 0.0) * w_scaled[2 * j + 1]
        admissible = ((base + row_iota) >> CHUNK_SHIFT) <= q_chunk
        key = _f32_key(jnp.where(admissible, acc, -jnp.inf))
        key_scr[pl.ds(base, KEY_CHUNK), :] = key
        ukey = lax.bitcast_convert_type(key, U32) ^ np.uint32(0x80000000)
        for gg in range(KEY_CHUNK // GROUP_ROWS):
            words = [ukey[gg * GROUP_ROWS + w * SUBLANES:gg * GROUP_ROWS + (w + 1) * SUBLANES] for w in range(32)]
            grp = pl.ds(pl.multiple_of(ci * (KEY_CHUNK // 32) + gg * SUBLANES, SUBLANES), SUBLANES)
            for b, plane in enumerate(_bit_planes(words)):
                plane_scr[b, grp, :] = plane

    def score_pair(t, carry):
        score_chunk(2 * t)
        score_chunk(2 * t + 1)
        return carry

    lax.fori_loop(0, n_chunks // 2, score_pair, 0)

    @pl.when(lax.rem(n_chunks, 2) == 1)
    def _():
        score_chunk(n_chunks - 1)

    def count_where(src_ref, pred):
        def body(ci, a):
            base = pl.multiple_of(ci * KEY_CHUNK, KEY_CHUNK)
            hit = pred(src_ref[pl.ds(base, KEY_CHUNK), :]).astype(I32)
            return a + jnp.sum(hit.reshape(KEY_CHUNK // SUBLANES, SUBLANES, LANES), axis=0)
        a = lax.fori_loop(0, n_chunks, body, jnp.zeros((SUBLANES, LANES), I32))
        return jnp.sum(a, axis=0, keepdims=True)

    n_groups = n_chunks * (KEY_CHUNK // GROUP_ROWS)
    max_groups = plane_scr.shape[1] // SUBLANES

    def lane_total(words):
        pc = [lax.population_count(w) for w in words]
        while len(pc) > 1:
            pc = [pc[k] + pc[k + 1] for k in range(0, len(pc) - 1, 2)] + ([pc[-1]] if len(pc) % 2 else [])
        return jnp.sum(lax.bitcast_convert_type(pc[0], I32), axis=0, keepdims=True)

    def choose_bit(b, carry):
        alive, need, prefix = carry
        planes = plane_scr[b]
        ones = [alive[g] & planes[g * SUBLANES:(g + 1) * SUBLANES] for g in range(max_groups)]
        cnt = lane_total(ones)
        take = cnt >= need
        alive = tuple(jnp.where(take, ones[g], alive[g] ^ ones[g]) for g in range(max_groups))
        bit = lax.shift_left(jnp.uint32(1), jnp.asarray(31 - b, U32))
        return alive, jnp.where(take, need, need - cnt), jnp.where(take, prefix | bit, prefix)

    alive0 = tuple(jnp.zeros((SUBLANES, LANES), U32) + jnp.where(g < n_groups, np.uint32(0xFFFFFFFF), np.uint32(0))
                   for g in range(max_groups))
    alive, need_left, prefix = lax.fori_loop(
        0, 32, choose_bit, (alive0, jnp.full((1, LANES), k_top, I32), jnp.zeros((1, LANES), U32)))
    thr = lax.bitcast_convert_type(prefix ^ np.uint32(0x80000000), I32)
    n_tied = lane_total(list(alive))

    tie = (n_tied > need_left) & (thr > KEY_NEG_INF) & (thr < KEY_POS_INF)
    any_tie = jnp.max(tie.astype(I32)) > 0

    def write_mask(select):
        def body(ci, carry):
            base = pl.multiple_of(ci * KEY_CHUNK, KEY_CHUNK)
            blk = key_scr[pl.ds(base, KEY_CHUNK), :]
            sel = select(blk, base) & (blk > KEY_NEG_INF) & (blk < KEY_POS_INF)
            mb_scr[pl.ds(base, KEY_CHUNK), :] = jnp.where(sel, 0.0, MASKED).astype(BF16)
            return carry
        lax.fori_loop(0, n_chunks, body, 0)

    @pl.when(jnp.logical_not(any_tie))
    def _():
        write_mask(lambda blk, base: blk >= thr)

    @pl.when(any_tie)
    def _break_ties():
        need = need_left

        def tied_index(ci, carry):
            base = pl.multiple_of(ci * KEY_CHUNK, KEY_CHUNK)
            tied = key_scr[pl.ds(base, KEY_CHUNK), :] == thr
            tid_scr[pl.ds(base, KEY_CHUNK), :] = jnp.where(tied, base + row_iota, INT_MAX)
            return carry

        lax.fori_loop(0, n_chunks, tied_index, 0)

        def grow(p, last):
            trial = last + lax.shift_left(jnp.int32(1), idx_bits - 1 - p)
            below = count_where(tid_scr, lambda blk: blk < trial)
            return jnp.where(below <= need - 1, trial, last)

        last = lax.fori_loop(0, idx_bits, grow, jnp.zeros((1, LANES), I32))
        last = jnp.where(tie, last, INT_MAX - 1)

        def body(ci, carry):
            base = pl.multiple_of(ci * KEY_CHUNK, KEY_CHUNK)
            blk = key_scr[pl.ds(base, KEY_CHUNK), :]
            sel = (blk > thr) | (tid_scr[pl.ds(base, KEY_CHUNK), :] <= last)
            sel = sel & (blk > KEY_NEG_INF) & (blk < KEY_POS_INF)
            mb_scr[pl.ds(base, KEY_CHUNK), :] = jnp.where(sel, 0.0, MASKED).astype(BF16)
            return carry

        lax.fori_loop(0, n_chunks, body, 0)

    blocks_per_chunk = KEY_CHUNK // TQ
    win_blk = jnp.maximum(i - (blocks_per_chunk - 1), 0)
    win0 = pl.multiple_of(win_blk * TQ, TQ)
    mbw_scr[...] = mb_scr[pl.ds(win0, KEY_CHUNK), :]
    mb_scr[pl.ds(win0, KEY_CHUNK), :] = jnp.full((KEY_CHUNK, LANES), MASKED, BF16)
    n_far = (win_blk + blocks_per_chunk - 1) // blocks_per_chunk

    def window_bias(h):
        tiles = []
        for t in range(blocks_per_chunk):
            d = i - (win_blk + t)
            tiles.append(bias_scr[jnp.where(d == 0, 0, jnp.where(d == 1, 1, 2)), h])
        return jnp.concatenate(tiles, axis=0)

    m_scr[...] = jnp.full((A_HEADS, TQ), M_INIT, F32)
    acc_scr[...] = jnp.zeros((A_HEADS, ACC_ROWS, TQ), F32)
    ones = jnp.ones((ACC_ROWS - A_HEAD_DIM, KEY_CHUNK), BF16)

    def stage_a(row0, mask, bias_of, s_buf, c_buf):
        for j in range(A_HEADS // 2):
            kc = jnp.concatenate([k_ref[pl.ds(row0, KEY_CHUNK), j * LANES:(j + 1) * LANES], mask], axis=1)
            s = lax.dot_general(kc, bq_scr[j], DN_LAST, preferred_element_type=F32)
            for hh in range(2):
                h = 2 * j + hh
                sh = s[:, hh * TQ:(hh + 1) * TQ]
                if bias_of is not None:
                    sh = sh + bias_of(h)
                s_buf[h] = sh
                c_buf[h:h + 1, :] = jnp.max(sh, axis=0, keepdims=True)

    def stage_a_far(ci, s_buf, c_buf):
        row0 = pl.multiple_of(ci * KEY_CHUNK, KEY_CHUNK)
        stage_a(row0, mb_scr[pl.ds(row0, KEY_CHUNK), :], None, s_buf, c_buf)

    def stage_b(blk0, s_buf, c_buf):
        for h in range(A_HEADS):
            m_old = m_scr[h:h + 1, :]
            m_new = jnp.maximum(m_old, c_buf[h:h + 1, :])
            p = jnp.exp2(s_buf[h] - m_new).astype(BF16)
            vt = [vt_ref[blk0 + t, h * A_HEAD_DIM:(h + 1) * A_HEAD_DIM, :] for t in range(blocks_per_chunk)]
            pv = jnp.dot(jnp.concatenate([jnp.concatenate(vt, axis=1), ones], axis=0), p,
                         preferred_element_type=F32)
            acc_scr[h] = jnp.exp2(m_old - m_new) * acc_scr[h] + pv
            m_scr[h:h + 1, :] = m_new

    stage_a(win0, mbw_scr[...], window_bias, sa_scr, ca_scr)

    def item_pair(t, carry):
        stage_a_far(2 * t, sb_scr, cb_scr)
        stage_b(jnp.where(t == 0, win_blk, (2 * t - 1) * blocks_per_chunk), sa_scr, ca_scr)

        @pl.when(2 * t + 1 < n_far)
        def _():
            stage_a_far(2 * t + 1, sa_scr, ca_scr)
            stage_b(2 * t * blocks_per_chunk, sb_scr, cb_scr)

        return carry

    lax.fori_loop(0, (n_far + 1) // 2, item_pair, 0)
    last_blk = jnp.where(n_far == 0, win_blk, (n_far - 1) * blocks_per_chunk)

    @pl.when(lax.rem(n_far, 2) == 1)
    def _():
        stage_b(last_blk, sb_scr, cb_scr)

    @pl.when(lax.rem(n_far, 2) == 0)
    def _():
        stage_b(last_blk, sa_scr, ca_scr)

    for h in range(A_HEADS):
        ot_scr[h * A_HEAD_DIM:(h + 1) * A_HEAD_DIM, :] = (
            acc_scr[h, 0:A_HEAD_DIM, :] / acc_scr[h, A_HEAD_DIM:A_HEAD_DIM + 1, :])

    out_ref[...] = ot_scr[...].T


def _t5_bucket(rel):
    half = REL_BUCKETS // 2
    max_exact = half // 2
    base = jnp.where(rel > 0, half, 0).astype(jnp.int32)
    n = jnp.abs(rel)
    nf = jnp.maximum(n, 1).astype(jnp.float32)
    large = max_exact + (jnp.log(nf / max_exact) / math.log(REL_MAX_DIST / max_exact)
                         * (half - max_exact)).astype(jnp.int32)
    large = jnp.minimum(large, half - 1)
    return base + jnp.where(n < max_exact, n, large)


def _attention(nat, wt, vt, rel_bias):
    bsz, seq, _ = nat.shape
    k_top = min(TOPK_MAX, seq // 4)
    idx_bits = max(1, int(math.ceil(math.log2(seq))))
    r = jnp.arange(TQ, dtype=jnp.int32)[:, None]
    t = jnp.arange(TQ, dtype=jnp.int32)[None, :]
    buckets = jnp.stack([_t5_bucket(r - t - TQ * d) for d in range(2)])
    return pl.pallas_call(
        functools.partial(_attn_kernel, k_top=k_top, idx_bits=idx_bits),
        grid=(bsz, seq // TQ),
        in_specs=[pl.BlockSpec(memory_space=pltpu.SMEM),
                  pl.BlockSpec((2, TQ, LANES), lambda b, i: (0, 0, 0)),
                  pl.BlockSpec((None, TQ, A_WIDTH), lambda b, i: (b, i, 0)),
                  pl.BlockSpec((None, TQ, A_WIDTH), lambda b, i: (b, i, 2)),
                  pl.BlockSpec((None, 16, TQ), lambda b, i: (b, 0, i)),
                  pl.BlockSpec((None, seq, LANES), lambda b, i: (b, 0, 3 * A_WIDTH // LANES)),
                  pl.BlockSpec((None, seq, A_WIDTH), lambda b, i: (b, 0, 1)),
                  pl.BlockSpec((None, seq // LANES, A_WIDTH, LANES), lambda b, i: (b, 0, 0, 0))],
        out_specs=pl.BlockSpec((None, TQ, A_WIDTH), lambda b, i: (b, i, 0)),
        out_shape=jax.ShapeDtypeStruct((bsz, seq, A_WIDTH), F32),
        scratch_shapes=[pltpu.VMEM((seq, LANES), I32),
                        pltpu.VMEM((32, seq // 32, LANES), U32),
                        pltpu.VMEM((seq, LANES), I32),
                        pltpu.VMEM((seq, LANES), BF16),
                        pltpu.VMEM((KEY_CHUNK, LANES), BF16),
                        pltpu.VMEM((3, A_HEADS, TQ, LANES), F32),
                        pltpu.VMEM((A_HEADS // 2, 2 * TQ, LANES), BF16),
                        pltpu.VMEM((A_HEADS // 2, 2 * TQ, 2 * LANES), BF16),
                        pltpu.VMEM((A_HEADS, KEY_CHUNK, TQ), F32),
                        pltpu.VMEM((A_HEADS, KEY_CHUNK, TQ), F32),
                        pltpu.VMEM((A_HEADS, TQ), F32),
                        pltpu.VMEM((A_HEADS, TQ), F32),
                        pltpu.VMEM((A_HEADS, TQ), F32),
                        pltpu.VMEM((A_HEADS, ACC_ROWS, TQ), F32),
                        pltpu.VMEM((A_WIDTH, TQ), F32)],
        compiler_params=pltpu.CompilerParams(dimension_semantics=("arbitrary", "arbitrary"),
                                             vmem_limit_bytes=VMEM_LIMIT),
        name="sparse_attn",
    )(rel_bias, buckets, nat, nat, wt, nat, nat, vt)


def _out_kernel(attn_ref, za_ref, cb_ref, ga_ref, gb_ref, x_ref, p_ref, wa_ref, wb_ref, wo_ref,
                pg_ref, wpg_ref, wpp_ref, fg_ref, out_ref):
    za = za_ref[...]
    ya_in = (attn_ref[...] * (za * _sigmoid(za))).astype(BF16)
    y_a = jnp.dot(ya_in, wa_ref[...], preferred_element_type=F32)
    y_b = jnp.dot(cb_ref[...], wb_ref[...], preferred_element_type=F32)
    merged = _sigmoid(ga_ref[...]) * y_a + _sigmoid(gb_ref[...]) * y_b
    x1 = x_ref[...] + jnp.dot(merged.astype(BF16), wo_ref[...], preferred_element_type=F32)
    e = jnp.dot(p_ref[...].astype(BF16), wpp_ref[...], preferred_element_type=F32)
    gate = _sigmoid(jnp.dot(_rms(x1, pg_ref[...]).astype(BF16), wpg_ref[...], preferred_element_type=F32))
    out_ref[...] = _rms(x1 + gate * e, fg_ref[...])


def _out_stage(attn, ew, cbr, x, p, wa, wb, wo, pg, wpg, wpp, fg, tm):
    bsz, seq, d = x.shape
    full = lambda a: pl.BlockSpec(a.shape, lambda b, s: (0,) * a.ndim)
    return pl.pallas_call(
        _out_kernel,
        grid=(bsz, seq // tm),
        in_specs=[pl.BlockSpec((None, tm, A_WIDTH), lambda b, s: (b, s, 0)),
                  pl.BlockSpec((None, tm, A_WIDTH), lambda b, s: (b, s, 2 * D_MODEL // A_WIDTH)),
                  pl.BlockSpec((None, tm, B_WIDTH), lambda b, s: (b, s, 0)),
                  pl.BlockSpec((None, tm, D_MODEL), lambda b, s: (b, s, 0)),
                  pl.BlockSpec((None, tm, D_MODEL), lambda b, s: (b, s, 1)),
                  pl.BlockSpec((None, tm, d), lambda b, s: (b, s, 0)),
                  pl.BlockSpec((None, tm, PLE_DIM), lambda b, s: (b, s, 0)),
                  full(wa), full(wb), full(wo), full(pg), full(wpg), full(wpp), full(fg)],
        out_specs=pl.BlockSpec((None, tm, d), lambda b, s: (b, s, 0)),
        out_shape=jax.ShapeDtypeStruct((bsz, seq, d), F32),
        compiler_params=pltpu.CompilerParams(dimension_semantics=("arbitrary", "arbitrary"),
                                             vmem_limit_bytes=VMEM_LIMIT),
        name="merge_out",
    )(attn, ew, cbr, ew, ew, x, p, wa, wb, wo, pg, wpg, wpp, fg)


def kernel(x, p, norm_in_g, w_in, conv_w, conv_b, conv_ln_g, conv_ln_b, w_branch_a, w_branch_b, w_out,
           ple_norm_g, w_ple_gate, w_ple_proj, rel_bias, final_norm_g):
    seq = x.shape[1]
    assert w_in.shape[0] == 1, "one layer: the fused output stage ends with the final norm"
    assert seq % KEY_CHUNK == 0 and seq >= 2 * TQ and x.shape[2] == D_MODEL
    offs = np.cumsum((0,) + IN_SIZES)
    row = lambda v: v.reshape(1, -1)
    for i in range(1):
        w = w_in[i]
        wq, wk, wv, wza, wqi, wki, wwi, wglu, wzb, wga, wgb = [w[:, offs[n]:offs[n + 1]] for n in range(11)]
        wnat = jnp.concatenate([wq, wk, wqi, wki, wki], axis=1).astype(BF16)
        wew = jnp.concatenate([wga, wgb, wza], axis=1).astype(BF16)
        wvt = wv.T.astype(BF16)
        wwt = jnp.pad(wwi.T, ((0, 16 - IDX_HEADS), (0, 0))).astype(BF16)
        nat, ew, vt, wt, cbr = _proj(x, row(norm_in_g[i]), wnat, wglu.astype(BF16), wzb.astype(BF16), wew, wvt,
                                     wwt, conv_w[i].reshape(CONV_K, B_WIDTH), row(conv_b[i]),
                                     row(conv_ln_g[i]), row(conv_ln_b[i]), tm=256)
        attn = _attention(nat, wt, vt, rel_bias)
        x = _out_stage(attn, ew, cbr, x, p[i], w_branch_a[i].astype(BF16), w_branch_b[i].astype(BF16),
                       w_out[i].astype(BF16), row(ple_norm_g[i]), w_ple_gate[i].astype(BF16),
                       w_ple_proj[i].astype(BF16), row(final_norm_g), tm=256)
    return x
```

```python
import functools
import math

import numpy as np
import jax
import jax.numpy as jnp
from jax import lax
from jax.experimental import pallas as pl
from jax.experimental.pallas import tpu as pltpu

D_MODEL = 1024
CHUNK = 64
CHUNK_SHIFT = 6
A_HEADS = 8
A_HEAD_DIM = 64
A_WIDTH = A_HEADS * A_HEAD_DIM
IDX_HEADS = 8
IDX_DIM = 64
TOPK_MAX = 256
REL_BUCKETS = 32
REL_MAX_DIST = 128
B_WIDTH = 512
CONV_K = 31
PLE_DIM = 256
EPS = 1e-6
IN_SIZES = (A_WIDTH, A_WIDTH, A_WIDTH, A_WIDTH, IDX_HEADS * IDX_DIM, IDX_DIM, IDX_HEADS,
            2 * B_WIDTH, B_WIDTH, D_MODEL, D_MODEL)

LANES = 128
SUBLANES = 8
TQ = 128
KEY_CHUNK = 512
ACC_ROWS = A_HEAD_DIM + 16
FAR_BUCKET = 15
CONV_HALO = 32
VMEM_LIMIT = 48 * 1024 * 1024

INT_MIN = -2 ** 31
INT_MAX = 2 ** 31 - 1
KEY_NEG_INF = (0xFF800000 ^ 0x7FFFFFFF) - 2 ** 32
KEY_POS_INF = 0x7F800000
GROUP_ROWS = 32 * SUBLANES
M_INIT = -3.0e38
MASKED = -1.0e30
LOG2E = math.log2(math.e)

F32 = jnp.float32
BF16 = jnp.bfloat16
I32 = jnp.int32
U32 = jnp.uint32
DN_LAST = (((1,), (1,)), ((), ()))


def _rms(x, g):
    return x * lax.rsqrt(jnp.mean(x * x, axis=-1, keepdims=True) + EPS) * g


def _sigmoid(x):
    return 1.0 / (1.0 + jnp.exp(-x))


def _proj_kernel(x_ref, g_ref, wnat_ref, wglu_ref, wzb_ref, wew_ref, wvt_ref, wwt_ref, cw_ref, cb_ref, lg_ref,
                 lb_ref, nat_ref, ew_ref, vt_ref, wt_ref, cbr_ref, ext_ref, *, row_tile):
    s = pl.program_id(1)
    tm = x_ref.shape[0]
    h = _rms(x_ref[...], g_ref[...]).astype(BF16)

    @pl.when(s == 0)
    def _():
        ext_ref[0:CONV_HALO, :] = jnp.zeros((CONV_HALO, B_WIDTH), F32)
        ext_ref[CONV_HALO + tm:, :] = jnp.zeros((SUBLANES, B_WIDTH), F32)

    @pl.when(s > 0)
    def _():
        ext_ref[0:CONV_HALO, :] = ext_ref[tm:tm + CONV_HALO, :]

    glu = jnp.dot(h, wglu_ref[...], preferred_element_type=F32)
    ext_ref[CONV_HALO:CONV_HALO + tm, :] = glu[:, :B_WIDTH] * _sigmoid(glu[:, B_WIDTH:])
    zb_all = jnp.dot(h, wzb_ref[...], preferred_element_type=F32)
    first = CONV_HALO - (CONV_K - 1)
    for r in range(tm // row_tile):
        r0 = r * row_tile
        acc = jnp.zeros((row_tile, B_WIDTH), F32)
        for res in range(SUBLANES):
            part = jnp.zeros((row_tile + SUBLANES, B_WIDTH), F32)
            for off in range(res, first + CONV_K, SUBLANES):
                if off >= first:
                    part = part + (ext_ref[r0 + off - res:r0 + off - res + row_tile + SUBLANES, :]
                                   * cw_ref[off - first:off - first + 1, :])
            acc = acc + part[res:res + row_tile]
        acc = acc + cb_ref[...]
        mu = jnp.mean(acc, axis=-1, keepdims=True)
        xc = acc - mu
        var = jnp.mean(xc * xc, axis=-1, keepdims=True)
        y = xc * lax.rsqrt(var + EPS) * lg_ref[...] + lb_ref[...]
        zb = zb_all[r0:r0 + row_tile]
        cbr_ref[r0:r0 + row_tile, :] = ((y * _sigmoid(y)) * (zb * _sigmoid(zb))).astype(BF16)

    nat_ref[...] = jnp.dot(h, wnat_ref[...], preferred_element_type=F32).astype(BF16)
    ew_ref[...] = jnp.dot(h, wew_ref[...], preferred_element_type=F32)
    vt = lax.dot_general(wvt_ref[...], h, DN_LAST, preferred_element_type=F32).astype(BF16)
    for c in range(vt_ref.shape[0]):
        vt_ref[c] = vt[:, c * LANES:(c + 1) * LANES]
    wt_ref[...] = lax.dot_general(wwt_ref[...], h, DN_LAST, preferred_element_type=F32)


def _proj(x, g, wnat, wglu, wzb, wew, wvt, wwt, cw, cb, lg, lb, tm):
    bsz, seq, d = x.shape
    n_nat, n_ew = wnat.shape[1], wew.shape[1]
    full = lambda a: pl.BlockSpec(a.shape, lambda b, s: (0,) * a.ndim)
    return pl.pallas_call(
        functools.partial(_proj_kernel, row_tile=64),
        grid=(bsz, seq // tm),
        in_specs=[pl.BlockSpec((None, tm, d), lambda b, s: (b, s, 0)),
                  full(g), full(wnat), full(wglu), full(wzb), full(wew), full(wvt), full(wwt),
                  full(cw), full(cb), full(lg), full(lb)],
        out_specs=[pl.BlockSpec((None, tm, n_nat), lambda b, s: (b, s, 0)),
                   pl.BlockSpec((None, tm, n_ew), lambda b, s: (b, s, 0)),
                   pl.BlockSpec((None, tm // LANES, A_WIDTH, LANES), lambda b, s: (b, s, 0, 0)),
                   pl.BlockSpec((None, 16, tm), lambda b, s: (b, 0, s)),
                   pl.BlockSpec((None, tm, B_WIDTH), lambda b, s: (b, s, 0))],
        out_shape=[jax.ShapeDtypeStruct((bsz, seq, n_nat), BF16),
                   jax.ShapeDtypeStruct((bsz, seq, n_ew), F32),
                   jax.ShapeDtypeStruct((bsz, seq // LANES, A_WIDTH, LANES), BF16),
                   jax.ShapeDtypeStruct((bsz, 16, seq), F32),
                   jax.ShapeDtypeStruct((bsz, seq, B_WIDTH), BF16)],
        scratch_shapes=[pltpu.VMEM((tm + CONV_HALO + SUBLANES, B_WIDTH), F32)],
        compiler_params=pltpu.CompilerParams(dimension_semantics=("arbitrary", "arbitrary"),
                                             vmem_limit_bytes=VMEM_LIMIT),
        name="in_proj_conv",
    )(x, g, wnat, wglu, wzb, wew, wvt, wwt, cw, cb, lg, lb)


def _f32_key(x):
    b = lax.bitcast_convert_type(x, I32)
    return b ^ ((b >> 31) & 0x7FFFFFFF)


def _bit_planes(words):
    a = list(words)
    j, m = 16, 0x0000FFFF
    while j:
        k = 0
        while k < 32:
            t = (a[k] ^ (a[k + j] >> np.uint32(j))) & np.uint32(m)
            a[k] = a[k] ^ t
            a[k + j] = a[k + j] ^ (t << np.uint32(j))
            k = (k + j + 1) & ~j
        j >>= 1
        m ^= (m << j) & 0xFFFFFFFF
    return a


def _attn_kernel(relb_ref, bucket_ref, q_ref, qi_ref, wt_ref, kidx_ref, k_ref, vt_ref, out_ref,
                 key_scr, plane_scr, mb_scr, mbw_scr, bias_scr, bidx_scr, bq_scr, sa_scr, sb_scr,
                 ca_scr, cb_scr, m_scr, acc_scr, ot_scr, *, k_top, idx_bits):
    i = pl.program_id(1)
    n_chunks = i // (KEY_CHUNK // TQ) + 1
    idx_scale = (IDX_DIM ** -0.5) * (IDX_HEADS ** -0.5)
    attn_scale = A_HEAD_DIM ** -0.5

    @pl.when(i == 0)
    def _init_bias():
        plane_scr[...] = jnp.zeros(plane_scr.shape, U32)
        for h in range(A_HEADS):
            bias_scr[2, h] = jnp.zeros((TQ, LANES), F32)
        for d in range(2):
            for h in range(A_HEADS):
                bias_scr[d, h] = jnp.zeros((TQ, LANES), F32) + relb_ref[0, h]

        def fill(b, carry):
            for d in range(2):
                hit = bucket_ref[d] == b
                for h in range(A_HEADS):
                    bias_scr[d, h] = jnp.where(hit, relb_ref[b, h], bias_scr[d, h])
            return carry

        lax.fori_loop(1, REL_BUCKETS, fill, 0)
        for d in range(2):
            for h in range(A_HEADS):
                bias_scr[d, h] = (bias_scr[d, h] - relb_ref[FAR_BUCKET, h]) * LOG2E
        eye = (lax.broadcasted_iota(I32, (2 * TQ, LANES), 0) & (TQ - 1)) == lax.broadcasted_iota(
            I32, (2 * TQ, LANES), 1)
        for j in range(A_HEADS // 2):
            bq_scr[j, :, LANES:] = jnp.where(eye, 1.0, 0.0).astype(BF16)

    lane = lax.broadcasted_iota(I32, (TQ, LANES), 1)
    low = lane < A_HEAD_DIM
    for j in range(A_HEADS // 2):
        qi = qi_ref[:, j * LANES:(j + 1) * LANES].astype(F32)
        bidx_scr[j, 0:TQ, :] = jnp.where(low, qi, 0.0).astype(BF16)
        bidx_scr[j, TQ:2 * TQ, :] = jnp.where(low, 0.0, qi).astype(BF16)
        qq = q_ref[:, j * LANES:(j + 1) * LANES].astype(F32) * (attn_scale * LOG2E)
        bq_scr[j, 0:TQ, 0:LANES] = jnp.where(low, qq, 0.0).astype(BF16)
        bq_scr[j, TQ:2 * TQ, 0:LANES] = jnp.where(low, 0.0, qq).astype(BF16)

    q_chunk = (i * TQ + lax.broadcasted_iota(I32, (1, LANES), 1)) >> CHUNK_SHIFT
    w_scaled = [wt_ref[h:h + 1, :] * idx_scale for h in range(IDX_HEADS)]
    row_iota = lax.broadcasted_iota(I32, (KEY_CHUNK, LANES), 0)

    def score_chunk(ci):
        base = pl.multiple_of(ci * KEY_CHUNK, KEY_CHUNK)
        kc = kidx_ref[pl.ds(base, KEY_CHUNK), :]
        acc = jnp.zeros((KEY_CHUNK, LANES), F32)
        for j in range(IDX_HEADS // 2):
            d = lax.dot_general(kc, bidx_scr[j], DN_LAST, preferred_element_type=F32)
            acc = acc + jnp.maximum(d[:, :TQ], 0.0) * w_scaled[2 * j]
            acc = acc + jnp.maximum(d[:, TQ:], 0.0) * w_scaled[2 * j + 1]
        admissible = ((base + row_iota) >> CHUNK_SHIFT) <= q_chunk
        key = _f32_key(jnp.where(admissible, acc, -jnp.inf))
        key_scr[pl.ds(base, KEY_CHUNK), :] = key
        ukey = lax.bitcast_convert_type(key, U32) ^ np.uint32(0x80000000)
        for gg in range(KEY_CHUNK // GROUP_ROWS):
            words = [ukey[gg * GROUP_ROWS + w * SUBLANES:gg * GROUP_ROWS + (w + 1) * SUBLANES] for w in range(32)]
            grp = pl.ds(pl.multiple_of(ci * (KEY_CHUNK // 32) + gg * SUBLANES, SUBLANES), SUBLANES)
            for b, plane in enumerate(_bit_planes(words)):
                plane_scr[b, grp, :] = plane

    def score_pair(t, carry):
        score_chunk(2 * t)
        score_chunk(2 * t + 1)
        return carry

    lax.fori_loop(0, n_chunks // 2, score_pair, 0)

    @pl.when(lax.rem(n_chunks, 2) == 1)
    def _():
        score_chunk(n_chunks - 1)

    n_groups = n_chunks * (KEY_CHUNK // GROUP_ROWS)
    max_groups = plane_scr.shape[1] // SUBLANES

    def lane_total(words):
        pc = [lax.population_count(w) for w in words]
        while len(pc) > 1:
            pc = [pc[k] + pc[k + 1] for k in range(0, len(pc) - 1, 2)] + ([pc[-1]] if len(pc) % 2 else [])
        return jnp.sum(lax.bitcast_convert_type(pc[0], I32), axis=0, keepdims=True)

    def choose_bit(b, carry):
        alive, need, prefix = carry
        planes = plane_scr[b]
        ones = [alive[g] & planes[g * SUBLANES:(g + 1) * SUBLANES] for g in range(max_groups)]
        cnt = lane_total(ones)
        take = cnt >= need
        alive = tuple(jnp.where(take, ones[g], alive[g] ^ ones[g]) for g in range(max_groups))
        bit = lax.shift_left(jnp.uint32(1), jnp.asarray(31 - b, U32))
        return alive, jnp.where(take, need, need - cnt), jnp.where(take, prefix | bit, prefix)

    alive0 = tuple(jnp.zeros((SUBLANES, LANES), U32) + jnp.where(g < n_groups, np.uint32(0xFFFFFFFF), np.uint32(0))
                   for g in range(max_groups))
    alive, need_left, prefix = lax.fori_loop(
        0, 32, choose_bit, (alive0, jnp.full((1, LANES), k_top, I32), jnp.zeros((1, LANES), U32)))
    thr = lax.bitcast_convert_type(prefix ^ np.uint32(0x80000000), I32)
    n_tied = lane_total(list(alive))

    tie = (n_tied > need_left) & (thr > KEY_NEG_INF) & (thr < KEY_POS_INF)
    any_tie = jnp.max(tie.astype(I32)) > 0

    def write_mask(select):
        def body(ci, carry):
            base = pl.multiple_of(ci * KEY_CHUNK, KEY_CHUNK)
            blk = key_scr[pl.ds(base, KEY_CHUNK), :]
            sel = select(blk, base) & (blk > KEY_NEG_INF) & (blk < KEY_POS_INF)
            mb_scr[pl.ds(base, KEY_CHUNK), :] = jnp.where(sel, 0.0, MASKED).astype(BF16)
            return carry
        lax.fori_loop(0, n_chunks, body, 0)

    @pl.when(jnp.logical_not(any_tie))
    def _():
        write_mask(lambda blk, base: blk >= thr)

    @pl.when(any_tie)
    def _break_ties():
        sub = lax.broadcasted_iota(I32, (SUBLANES, LANES), 0)
        all_ones = np.uint32(0xFFFFFFFF)

        def tied_below(row):
            parts = []
            for g in range(max_groups):
                r = row - g * GROUP_ROWS
                words = jnp.clip(r >> 3, 0, 32)
                top = ~lax.shift_right_logical(jnp.zeros((1, LANES), U32) + all_ones,
                                               jnp.minimum(words, 31).astype(U32))
                top = jnp.where(words >= 32, all_ones, top)
                edge_bit = lax.shift_left(jnp.zeros((1, LANES), U32) + np.uint32(1),
                                          (31 - jnp.minimum(words, 31)).astype(U32))
                edge = (sub < (r & (SUBLANES - 1))) & (r >= 0) & (words < 32)
                parts.append(alive[g] & (top | jnp.where(edge, edge_bit, np.uint32(0))))
            return lane_total(parts)

        def grow(p, last):
            trial = last + lax.shift_left(jnp.int32(1), idx_bits - 1 - p)
            return jnp.where(tied_below(trial) <= need_left - 1, trial, last)

        last = lax.fori_loop(0, idx_bits, grow, jnp.zeros((1, LANES), I32))
        last = jnp.where(tie, last, INT_MAX)
        write_mask(lambda blk, base: (blk > thr) | ((blk == thr) & ((base + row_iota) <= last)))

    blocks_per_chunk = KEY_CHUNK // TQ
    win_blk = jnp.maximum(i - (blocks_per_chunk - 1), 0)
    win0 = pl.multiple_of(win_blk * TQ, TQ)
    mbw_scr[...] = mb_scr[pl.ds(win0, KEY_CHUNK), :]
    mb_scr[pl.ds(win0, KEY_CHUNK), :] = jnp.full((KEY_CHUNK, LANES), MASKED, BF16)
    n_far = (win_blk + blocks_per_chunk - 1) // blocks_per_chunk

    def window_bias(h):
        tiles = []
        for t in range(blocks_per_chunk):
            d = i - (win_blk + t)
            tiles.append(bias_scr[jnp.where(d == 0, 0, jnp.where(d == 1, 1, 2)), h])
        return jnp.concatenate(tiles, axis=0)

    m_scr[...] = jnp.full((A_HEADS, TQ), M_INIT, F32)
    acc_scr[...] = jnp.zeros((A_HEADS, ACC_ROWS, TQ), F32)
    ones = jnp.ones((ACC_ROWS - A_HEAD_DIM, KEY_CHUNK), BF16)

    def stage_a(row0, mask, bias_of, s_buf, c_buf):
        for j in range(A_HEADS // 2):
            kc = jnp.concatenate([k_ref[pl.ds(row0, KEY_CHUNK), j * LANES:(j + 1) * LANES], mask], axis=1)
            s = lax.dot_general(kc, bq_scr[j], DN_LAST, preferred_element_type=F32)
            for hh in range(2):
                h = 2 * j + hh
                sh = s[:, hh * TQ:(hh + 1) * TQ]
                if bias_of is not None:
                    sh = sh + bias_of(h)
                s_buf[h] = sh
                c_buf[h:h + 1, :] = jnp.max(sh, axis=0, keepdims=True)

    def stage_a_far(ci, s_buf, c_buf):
        row0 = pl.multiple_of(ci * KEY_CHUNK, KEY_CHUNK)
        stage_a(row0, mb_scr[pl.ds(row0, KEY_CHUNK), :], None, s_buf, c_buf)

    def stage_b(blk0, s_buf, c_buf):
        for h in range(A_HEADS):
            m_old = m_scr[h:h + 1, :]
            m_new = jnp.maximum(m_old, c_buf[h:h + 1, :])
            p = jnp.exp2(s_buf[h] - m_new).astype(BF16)
            vt = [vt_ref[blk0 + t, h * A_HEAD_DIM:(h + 1) * A_HEAD_DIM, :] for t in range(blocks_per_chunk)]
            pv = jnp.dot(jnp.concatenate([jnp.concatenate(vt, axis=1), ones], axis=0), p,
                         preferred_element_type=F32)
            acc_scr[h] = jnp.exp2(m_old - m_new) * acc_scr[h] + pv
            m_scr[h:h + 1, :] = m_new

    stage_a(win0, mbw_scr[...], window_bias, sa_scr, ca_scr)

    def item_pair(t, carry):
        stage_a_far(2 * t, sb_scr, cb_scr)
        stage_b(jnp.where(t == 0, win_blk, (2 * t - 1) * blocks_per_chunk), sa_scr, ca_scr)

        @pl.when(2 * t + 1 < n_far)
        def _():
            stage_a_far(2 * t + 1, sa_scr, ca_scr)
            stage_b(2 * t * blocks_per_chunk, sb_scr, cb_scr)

        return carry

    lax.fori_loop(0, (n_far + 1) // 2, item_pair, 0)
    last_blk = jnp.where(n_far == 0, win_blk, (n_far - 1) * blocks_per_chunk)

    @pl.when(lax.rem(n_far, 2) == 1)
    def _():
        stage_b(last_blk, sb_scr, cb_scr)

    @pl.when(lax.rem(n_far, 2) == 0)
    def _():
        stage_b(last_blk, sa_scr, ca_scr)

    for h in range(A_HEADS):
        ot_scr[h * A_HEAD_DIM:(h + 1) * A_HEAD_DIM, :] = (
            acc_scr[h, 0:A_HEAD_DIM, :] / acc_scr[h, A_HEAD_DIM:A_HEAD_DIM + 1, :])

    out_ref[...] = ot_scr[...].T


def _t5_bucket(rel):
    half = REL_BUCKETS // 2
    max_exact = half // 2
    base = jnp.where(rel > 0, half, 0).astype(jnp.int32)
    n = jnp.abs(rel)
    nf = jnp.maximum(n, 1).astype(jnp.float32)
    large = max_exact + (jnp.log(nf / max_exact) / math.log(REL_MAX_DIST / max_exact)
                         * (half - max_exact)).astype(jnp.int32)
    large = jnp.minimum(large, half - 1)
    return base + jnp.where(n < max_exact, n, large)


def _attention(nat, wt, vt, rel_bias):
    bsz, seq, _ = nat.shape
    k_top = min(TOPK_MAX, seq // 4)
    idx_bits = max(1, int(math.ceil(math.log2(seq))))
    r = jnp.arange(TQ, dtype=jnp.int32)[:, None]
    t = jnp.arange(TQ, dtype=jnp.int32)[None, :]
    buckets = jnp.stack([_t5_bucket(r - t - TQ * d) for d in range(2)])
    return pl.pallas_call(
        functools.partial(_attn_kernel, k_top=k_top, idx_bits=idx_bits),
        grid=(bsz, seq // TQ),
        in_specs=[pl.BlockSpec(memory_space=pltpu.SMEM),
                  pl.BlockSpec((2, TQ, LANES), lambda b, i: (0, 0, 0)),
                  pl.BlockSpec((None, TQ, A_WIDTH), lambda b, i: (b, i, 0)),
                  pl.BlockSpec((None, TQ, A_WIDTH), lambda b, i: (b, i, 2)),
                  pl.BlockSpec((None, 16, TQ), lambda b, i: (b, 0, i)),
                  pl.BlockSpec((None, seq, LANES), lambda b, i: (b, 0, 3 * A_WIDTH // LANES)),
                  pl.BlockSpec((None, seq, A_WIDTH), lambda b, i: (b, 0, 1)),
                  pl.BlockSpec((None, seq // LANES, A_WIDTH, LANES), lambda b, i: (b, 0, 0, 0))],
        out_specs=pl.BlockSpec((None, TQ, A_WIDTH), lambda b, i: (b, i, 0)),
        out_shape=jax.ShapeDtypeStruct((bsz, seq, A_WIDTH), F32),
        scratch_shapes=[pltpu.VMEM((seq, LANES), I32),
                        pltpu.VMEM((32, seq // 32, LANES), U32),
                        pltpu.VMEM((seq, LANES), BF16),
                        pltpu.VMEM((KEY_CHUNK, LANES), BF16),
                        pltpu.VMEM((3, A_HEADS, TQ, LANES), F32),
                        pltpu.VMEM((A_HEADS // 2, 2 * TQ, LANES), BF16),
                        pltpu.VMEM((A_HEADS // 2, 2 * TQ, 2 * LANES), BF16),
                        pltpu.VMEM((A_HEADS, KEY_CHUNK, TQ), F32),
                        pltpu.VMEM((A_HEADS, KEY_CHUNK, TQ), F32),
                        pltpu.VMEM((A_HEADS, TQ), F32),
                        pltpu.VMEM((A_HEADS, TQ), F32),
                        pltpu.VMEM((A_HEADS, TQ), F32),
                        pltpu.VMEM((A_HEADS, ACC_ROWS, TQ), F32),
                        pltpu.VMEM((A_WIDTH, TQ), F32)],
        compiler_params=pltpu.CompilerParams(dimension_semantics=("arbitrary", "arbitrary"),
                                             vmem_limit_bytes=VMEM_LIMIT),
        name="sparse_attn",
    )(rel_bias, buckets, nat, nat, wt, nat, nat, vt)


def _out_kernel(attn_ref, za_ref, cb_ref, ga_ref, gb_ref, x_ref, p_ref, wa_ref, wb_ref, wo_ref,
                pg_ref, wpg_ref, wpp_ref, fg_ref, out_ref):
    za = za_ref[...]
    ya_in = (attn_ref[...] * (za * _sigmoid(za))).astype(BF16)
    y_a = jnp.dot(ya_in, wa_ref[...], preferred_element_type=F32)
    y_b = jnp.dot(cb_ref[...], wb_ref[...], preferred_element_type=F32)
    merged = _sigmoid(ga_ref[...]) * y_a + _sigmoid(gb_ref[...]) * y_b
    x1 = x_ref[...] + jnp.dot(merged.astype(BF16), wo_ref[...], preferred_element_type=F32)
    e = jnp.dot(p_ref[...].astype(BF16), wpp_ref[...], preferred_element_type=F32)
    gate = _sigmoid(jnp.dot(_rms(x1, pg_ref[...]).astype(BF16), wpg_ref[...], preferred_element_type=F32))
    out_ref[...] = _rms(x1 + gate * e, fg_ref[...])


def _out_stage(attn, ew, cbr, x, p, wa, wb, wo, pg, wpg, wpp, fg, tm):
    bsz, seq, d = x.shape
    full = lambda a: pl.BlockSpec(a.shape, lambda b, s: (0,) * a.ndim)
    return pl.pallas_call(
        _out_kernel,
        grid=(bsz, seq // tm),
        in_specs=[pl.BlockSpec((None, tm, A_WIDTH), lambda b, s: (b, s, 0)),
                  pl.BlockSpec((None, tm, A_WIDTH), lambda b, s: (b, s, 2 * D_MODEL // A_WIDTH)),
                  pl.BlockSpec((None, tm, B_WIDTH), lambda b, s: (b, s, 0)),
                  pl.BlockSpec((None, tm, D_MODEL), lambda b, s: (b, s, 0)),
                  pl.BlockSpec((None, tm, D_MODEL), lambda b, s: (b, s, 1)),
                  pl.BlockSpec((None, tm, d), lambda b, s: (b, s, 0)),
                  pl.BlockSpec((None, tm, PLE_DIM), lambda b, s: (b, s, 0)),
                  full(wa), full(wb), full(wo), full(pg), full(wpg), full(wpp), full(fg)],
        out_specs=pl.BlockSpec((None, tm, d), lambda b, s: (b, s, 0)),
        out_shape=jax.ShapeDtypeStruct((bsz, seq, d), F32),
        compiler_params=pltpu.CompilerParams(dimension_semantics=("arbitrary", "arbitrary"),
                                             vmem_limit_bytes=VMEM_LIMIT),
        name="merge_out",
    )(attn, ew, cbr, ew, ew, x, p, wa, wb, wo, pg, wpg, wpp, fg)


def kernel(x, p, norm_in_g, w_in, conv_w, conv_b, conv_ln_g, conv_ln_b, w_branch_a, w_branch_b, w_out,
           ple_norm_g, w_ple_gate, w_ple_proj, rel_bias, final_norm_g):
    seq = x.shape[1]
    assert w_in.shape[0] == 1, "one layer: the fused output stage ends with the final norm"
    assert seq % KEY_CHUNK == 0 and seq >= 2 * TQ and x.shape[2] == D_MODEL
    offs = np.cumsum((0,) + IN_SIZES)
    row = lambda v: v.reshape(1, -1)
    for i in range(1):
        w = w_in[i]
        wq, wk, wv, wza, wqi, wki, wwi, wglu, wzb, wga, wgb = [w[:, offs[n]:offs[n + 1]] for n in range(11)]
        wnat = jnp.concatenate([wq, wk, wqi, wki, wki], axis=1).astype(BF16)
        wew = jnp.concatenate([wga, wgb, wza], axis=1).astype(BF16)
        wvt = wv.T.astype(BF16)
        wwt = jnp.pad(wwi.T, ((0, 16 - IDX_HEADS), (0, 0))).astype(BF16)
        nat, ew, vt, wt, cbr = _proj(x, row(norm_in_g[i]), wnat, wglu.astype(BF16), wzb.astype(BF16), wew, wvt,
                                     wwt, conv_w[i].reshape(CONV_K, B_WIDTH), row(conv_b[i]),
                                     row(conv_ln_g[i]), row(conv_ln_b[i]), tm=256)
        attn = _attention(nat, wt, vt, rel_bias)
        x = _out_stage(attn, ew, cbr, x, p[i], w_branch_a[i].astype(BF16), w_branch_b[i].astype(BF16),
                       w_out[i].astype(BF16), row(ple_norm_g[i]), w_ple_gate[i].astype(BF16),
                       w_ple_proj[i].astype(BF16), row(final_norm_g), tm=512)
    return x
```

```python
import functools
import math

import numpy as np
import jax
import jax.numpy as jnp
from jax import lax
from jax.experimental import pallas as pl
from jax.experimental.pallas import tpu as pltpu

D_MODEL = 1024
CHUNK = 64
CHUNK_SHIFT = 6
A_HEADS = 8
A_HEAD_DIM = 64
A_WIDTH = A_HEADS * A_HEAD_DIM
IDX_HEADS = 8
IDX_DIM = 64
TOPK_MAX = 256
REL_BUCKETS = 32
REL_MAX_DIST = 128
B_WIDTH = 512
CONV_K = 31
PLE_DIM = 256
EPS = 1e-6
IN_SIZES = (A_WIDTH, A_WIDTH, A_WIDTH, A_WIDTH, IDX_HEADS * IDX_DIM, IDX_DIM, IDX_HEADS,
            2 * B_WIDTH, B_WIDTH, D_MODEL, D_MODEL)

LANES = 128
SUBLANES = 8
TQ = 128
KEY_CHUNK = 512
ACC_ROWS = A_HEAD_DIM + 16
FAR_BUCKET = 15
CONV_HALO = 32
VMEM_LIMIT = 48 * 1024 * 1024

INT_MIN = -2 ** 31
INT_MAX = 2 ** 31 - 1
KEY_NEG_INF = (0xFF800000 ^ 0x7FFFFFFF) - 2 ** 32
KEY_POS_INF = 0x7F800000
GROUP_ROWS = 32 * SUBLANES
M_INIT = -3.0e38
MASKED = -1.0e30
LOG2E = math.log2(math.e)

F32 = jnp.float32
BF16 = jnp.bfloat16
I32 = jnp.int32
U32 = jnp.uint32
DN_LAST = (((1,), (1,)), ((), ()))


def _rms(x, g):
    return x * lax.rsqrt(jnp.mean(x * x, axis=-1, keepdims=True) + EPS) * g


def _sigmoid(x):
    return 1.0 / (1.0 + jnp.exp(-x))


def _proj_kernel(x_ref, g_ref, wnat_ref, wglu_ref, wzb_ref, wew_ref, wvt_ref, wwt_ref, cw_ref, cb_ref, lg_ref,
                 lb_ref, nat_ref, ew_ref, vt_ref, wt_ref, cbr_ref, ext_ref, *, row_tile):
    s = pl.program_id(1)
    tm = x_ref.shape[0]
    h = _rms(x_ref[...], g_ref[...]).astype(BF16)

    @pl.when(s == 0)
    def _():
        ext_ref[0:CONV_HALO, :] = jnp.zeros((CONV_HALO, B_WIDTH), F32)
        ext_ref[CONV_HALO + tm:, :] = jnp.zeros((SUBLANES, B_WIDTH), F32)

    @pl.when(s > 0)
    def _():
        ext_ref[0:CONV_HALO, :] = ext_ref[tm:tm + CONV_HALO, :]

    glu = jnp.dot(h, wglu_ref[...], preferred_element_type=F32)
    ext_ref[CONV_HALO:CONV_HALO + tm, :] = glu[:, :B_WIDTH] * _sigmoid(glu[:, B_WIDTH:])
    zb_all = jnp.dot(h, wzb_ref[...], preferred_element_type=F32)
    first = CONV_HALO - (CONV_K - 1)
    for r in range(tm // row_tile):
        r0 = r * row_tile
        acc = jnp.zeros((row_tile, B_WIDTH), F32)
        for res in range(SUBLANES):
            part = jnp.zeros((row_tile + SUBLANES, B_WIDTH), F32)
            for off in range(res, first + CONV_K, SUBLANES):
                if off >= first:
                    tap = cw_ref[(off - first) * SUBLANES:(off - first + 1) * SUBLANES, :]
                    part = part + (ext_ref[r0 + off - res:r0 + off - res + row_tile + SUBLANES, :]
                                   * jnp.concatenate([tap] * (row_tile // SUBLANES + 1), axis=0))
            acc = acc + part[res:res + row_tile]
        acc = acc + cb_ref[...]
        mu = jnp.mean(acc, axis=-1, keepdims=True)
        xc = acc - mu
        var = jnp.mean(xc * xc, axis=-1, keepdims=True)
        y = xc * lax.rsqrt(var + EPS) * lg_ref[...] + lb_ref[...]
        zb = zb_all[r0:r0 + row_tile]
        cbr_ref[r0:r0 + row_tile, :] = ((y * _sigmoid(y)) * (zb * _sigmoid(zb))).astype(BF16)

    nat_ref[...] = jnp.dot(h, wnat_ref[...], preferred_element_type=F32).astype(BF16)
    ew_ref[...] = jnp.dot(h, wew_ref[...], preferred_element_type=F32)
    vt = lax.dot_general(wvt_ref[...], h, DN_LAST, preferred_element_type=F32).astype(BF16)
    for c in range(vt_ref.shape[0]):
        vt_ref[c] = vt[:, c * LANES:(c + 1) * LANES]
    wt_ref[...] = lax.dot_general(wwt_ref[...], h, DN_LAST, preferred_element_type=F32)


def _proj(x, g, wnat, wglu, wzb, wew, wvt, wwt, cw, cb, lg, lb, tm):
    bsz, seq, d = x.shape
    n_nat, n_ew = wnat.shape[1], wew.shape[1]
    full = lambda a: pl.BlockSpec(a.shape, lambda b, s: (0,) * a.ndim)
    return pl.pallas_call(
        functools.partial(_proj_kernel, row_tile=64),
        grid=(bsz, seq // tm),
        in_specs=[pl.BlockSpec((None, tm, d), lambda b, s: (b, s, 0)),
                  full(g), full(wnat), full(wglu), full(wzb), full(wew), full(wvt), full(wwt),
                  full(cw), full(cb), full(lg), full(lb)],
        out_specs=[pl.BlockSpec((None, tm, n_nat), lambda b, s: (b, s, 0)),
                   pl.BlockSpec((None, tm, n_ew), lambda b, s: (b, s, 0)),
                   pl.BlockSpec((None, tm // LANES, A_WIDTH, LANES), lambda b, s: (b, s, 0, 0)),
                   pl.BlockSpec((None, 16, tm), lambda b, s: (b, 0, s)),
                   pl.BlockSpec((None, tm, B_WIDTH), lambda b, s: (b, s, 0))],
        out_shape=[jax.ShapeDtypeStruct((bsz, seq, n_nat), BF16),
                   jax.ShapeDtypeStruct((bsz, seq, n_ew), F32),
                   jax.ShapeDtypeStruct((bsz, seq // LANES, A_WIDTH, LANES), BF16),
                   jax.ShapeDtypeStruct((bsz, 16, seq), F32),
                   jax.ShapeDtypeStruct((bsz, seq, B_WIDTH), BF16)],
        scratch_shapes=[pltpu.VMEM((tm + CONV_HALO + SUBLANES, B_WIDTH), F32)],
        compiler_params=pltpu.CompilerParams(dimension_semantics=("arbitrary", "arbitrary"),
                                             vmem_limit_bytes=VMEM_LIMIT),
        name="in_proj_conv",
    )(x, g, wnat, wglu, wzb, wew, wvt, wwt, cw, cb, lg, lb)


def _f32_key(x):
    b = lax.bitcast_convert_type(x, I32)
    return b ^ ((b >> 31) & 0x7FFFFFFF)


def _bit_planes(words):
    a = list(words)
    j, m = 16, 0x0000FFFF
    while j:
        k = 0
        while k < 32:
            t = (a[k] ^ (a[k + j] >> np.uint32(j))) & np.uint32(m)
            a[k] = a[k] ^ t
            a[k + j] = a[k + j] ^ (t << np.uint32(j))
            k = (k + j + 1) & ~j
        j >>= 1
        m ^= (m << j) & 0xFFFFFFFF
    return a


def _attn_kernel(relb_ref, bucket_ref, q_ref, qi_ref, wt_ref, kidx_ref, k_ref, vt_ref, out_ref,
                 key_scr, plane_scr, mb_scr, mbw_scr, bias_scr, bidx_scr, bq_scr, sa_scr, sb_scr,
                 ca_scr, cb_scr, m_scr, acc_scr, ot_scr, *, k_top, idx_bits):
    i = pl.program_id(1)
    n_chunks = i // (KEY_CHUNK // TQ) + 1
    idx_scale = (IDX_DIM ** -0.5) * (IDX_HEADS ** -0.5)
    attn_scale = A_HEAD_DIM ** -0.5

    @pl.when(i == 0)
    def _init_bias():
        plane_scr[...] = jnp.zeros(plane_scr.shape, U32)
        for h in range(A_HEADS):
            bias_scr[2, h] = jnp.zeros((TQ, LANES), F32)
        for d in range(2):
            for h in range(A_HEADS):
                bias_scr[d, h] = jnp.zeros((TQ, LANES), F32) + relb_ref[0, h]

        def fill(b, carry):
            for d in range(2):
                hit = bucket_ref[d] == b
                for h in range(A_HEADS):
                    bias_scr[d, h] = jnp.where(hit, relb_ref[b, h], bias_scr[d, h])
            return carry

        lax.fori_loop(1, REL_BUCKETS, fill, 0)
        for d in range(2):
            for h in range(A_HEADS):
                bias_scr[d, h] = (bias_scr[d, h] - relb_ref[FAR_BUCKET, h]) * LOG2E
        eye = (lax.broadcasted_iota(I32, (2 * TQ, LANES), 0) & (TQ - 1)) == lax.broadcasted_iota(
            I32, (2 * TQ, LANES), 1)
        for j in range(A_HEADS // 2):
            bq_scr[j, :, LANES:] = jnp.where(eye, 1.0, 0.0).astype(BF16)

    lane = lax.broadcasted_iota(I32, (TQ, LANES), 1)
    low = lane < A_HEAD_DIM
    for j in range(A_HEADS // 2):
        qi = qi_ref[:, j * LANES:(j + 1) * LANES].astype(F32)
        bidx_scr[j, 0:TQ, :] = jnp.where(low, qi, 0.0).astype(BF16)
        bidx_scr[j, TQ:2 * TQ, :] = jnp.where(low, 0.0, qi).astype(BF16)
        qq = q_ref[:, j * LANES:(j + 1) * LANES].astype(F32) * (attn_scale * LOG2E)
        bq_scr[j, 0:TQ, 0:LANES] = jnp.where(low, qq, 0.0).astype(BF16)
        bq_scr[j, TQ:2 * TQ, 0:LANES] = jnp.where(low, 0.0, qq).astype(BF16)

    q_chunk = (i * TQ + lax.broadcasted_iota(I32, (1, LANES), 1)) >> CHUNK_SHIFT
    w_scaled = [wt_ref[h:h + 1, :] * idx_scale for h in range(IDX_HEADS)]
    row_iota = lax.broadcasted_iota(I32, (KEY_CHUNK, LANES), 0)

    def score_rows(base, rows, masked):
        kc = kidx_ref[pl.ds(base, rows), :]
        acc = jnp.zeros((rows, LANES), F32)
        for j in range(IDX_HEADS // 2):
            d = lax.dot_general(kc, bidx_scr[j], DN_LAST, preferred_element_type=F32)
            acc = acc + jnp.maximum(d[:, :TQ], 0.0) * w_scaled[2 * j]
            acc = acc + jnp.maximum(d[:, TQ:], 0.0) * w_scaled[2 * j + 1]
        if masked:
            acc = jnp.where(((base + row_iota[0:rows]) >> CHUNK_SHIFT) <= q_chunk, acc, -jnp.inf)
        bits = lax.bitcast_convert_type(acc, I32)
        ukey = lax.bitcast_convert_type(bits ^ ((bits >> 31) | INT_MIN), U32)
        key_scr[pl.ds(base, rows), :] = lax.bitcast_convert_type(ukey ^ np.uint32(0x80000000), I32)
        for gg in range(rows // GROUP_ROWS):
            words = [ukey[gg * GROUP_ROWS + w * SUBLANES:gg * GROUP_ROWS + (w + 1) * SUBLANES] for w in range(32)]
            grp = pl.ds(pl.multiple_of(base // 32 + gg * SUBLANES, SUBLANES), SUBLANES)
            for b, plane in enumerate(_bit_planes(words)):
                plane_scr[b, grp, :] = plane

    def score_chunk(ci):
        score_rows(pl.multiple_of(ci * KEY_CHUNK, KEY_CHUNK), KEY_CHUNK, masked=False)

    def score_pair(t, carry):
        score_chunk(2 * t)
        score_chunk(2 * t + 1)
        return carry

    lax.fori_loop(0, (n_chunks - 1) // 2, score_pair, 0)

    @pl.when(lax.rem(n_chunks - 1, 2) == 1)
    def _():
        score_chunk(n_chunks - 2)

    last0 = pl.multiple_of((n_chunks - 1) * KEY_CHUNK, KEY_CHUNK)
    half = KEY_CHUNK // 2
    second_half = lax.rem(i, KEY_CHUNK // TQ) >= half // TQ

    @pl.when(second_half)
    def _():
        score_rows(last0, KEY_CHUNK, masked=True)

    @pl.when(jnp.logical_not(second_half))
    def _():
        score_rows(last0, half, masked=True)
        key_scr[pl.ds(pl.multiple_of(last0 + half, half), half), :] = jnp.full((half, LANES), KEY_NEG_INF, I32)

    n_groups = (n_chunks - 1) * (KEY_CHUNK // GROUP_ROWS) + jnp.where(second_half, 2, 1) * (half // GROUP_ROWS)
    max_groups = plane_scr.shape[1] // SUBLANES

    def lane_total(words):
        pc = [lax.population_count(w) for w in words]
        while len(pc) > 1:
            pc = [pc[k] + pc[k + 1] for k in range(0, len(pc) - 1, 2)] + ([pc[-1]] if len(pc) % 2 else [])
        return jnp.sum(lax.bitcast_convert_type(pc[0], I32), axis=0, keepdims=True)

    def choose_bit(b, carry):
        alive, need, prefix = carry
        planes = plane_scr[b]
        ones = [alive[g] & planes[g * SUBLANES:(g + 1) * SUBLANES] for g in range(max_groups)]
        cnt = lane_total(ones)
        take = cnt >= need
        alive = tuple(jnp.where(take, ones[g], alive[g] ^ ones[g]) for g in range(max_groups))
        bit = lax.shift_left(jnp.uint32(1), jnp.asarray(31 - b, U32))
        return alive, jnp.where(take, need, need - cnt), jnp.where(take, prefix | bit, prefix)

    alive0 = tuple(jnp.zeros((SUBLANES, LANES), U32) + jnp.where(g < n_groups, np.uint32(0xFFFFFFFF), np.uint32(0))
                   for g in range(max_groups))
    alive, need_left, prefix = lax.fori_loop(
        0, 32, choose_bit, (alive0, jnp.full((1, LANES), k_top, I32), jnp.zeros((1, LANES), U32)))
    thr = lax.bitcast_convert_type(prefix ^ np.uint32(0x80000000), I32)
    n_tied = lane_total(list(alive))

    tie = (n_tied > need_left) & (thr > KEY_NEG_INF) & (thr < KEY_POS_INF)
    any_tie = jnp.max(tie.astype(I32)) > 0

    def write_mask(select):
        def body(ci, carry):
            base = pl.multiple_of(ci * KEY_CHUNK, KEY_CHUNK)
            blk = key_scr[pl.ds(base, KEY_CHUNK), :]
            sel = select(blk, base) & (blk > KEY_NEG_INF) & (blk < KEY_POS_INF)
            mb_scr[pl.ds(base, KEY_CHUNK), :] = jnp.where(sel, 0.0, MASKED).astype(BF16)
            return carry
        lax.fori_loop(0, n_chunks, body, 0)

    @pl.when(jnp.logical_not(any_tie))
    def _():
        write_mask(lambda blk, base: blk >= thr)

    @pl.when(any_tie)
    def _break_ties():
        sub = lax.broadcasted_iota(I32, (SUBLANES, LANES), 0)
        all_ones = np.uint32(0xFFFFFFFF)

        def tied_below(row):
            parts = []
            for g in range(max_groups):
                r = row - g * GROUP_ROWS
                words = jnp.clip(r >> 3, 0, 32)
                top = ~lax.shift_right_logical(jnp.zeros((1, LANES), U32) + all_ones,
                                               jnp.minimum(words, 31).astype(U32))
                top = jnp.where(words >= 32, all_ones, top)
                edge_bit = lax.shift_left(jnp.zeros((1, LANES), U32) + np.uint32(1),
                                          (31 - jnp.minimum(words, 31)).astype(U32))
                edge = (sub < (r & (SUBLANES - 1))) & (r >= 0) & (words < 32)
                parts.append(alive[g] & (top | jnp.where(edge, edge_bit, np.uint32(0))))
            return lane_total(parts)

        def grow(p, last):
            trial = last + lax.shift_left(jnp.int32(1), idx_bits - 1 - p)
            return jnp.where(tied_below(trial) <= need_left - 1, trial, last)

        last = lax.fori_loop(0, idx_bits, grow, jnp.zeros((1, LANES), I32))
        last = jnp.where(tie, last, INT_MAX)
        write_mask(lambda blk, base: (blk > thr) | ((blk == thr) & ((base + row_iota) <= last)))

    blocks_per_chunk = KEY_CHUNK // TQ
    win_blk = jnp.maximum(i - (blocks_per_chunk - 1), 0)
    win0 = pl.multiple_of(win_blk * TQ, TQ)
    mbw_scr[...] = mb_scr[pl.ds(win0, KEY_CHUNK), :]
    mb_scr[pl.ds(win0, KEY_CHUNK), :] = jnp.full((KEY_CHUNK, LANES), MASKED, BF16)
    n_far = (win_blk + blocks_per_chunk - 1) // blocks_per_chunk

    def window_bias(h):
        tiles = []
        for t in range(blocks_per_chunk):
            d = i - (win_blk + t)
            tiles.append(bias_scr[jnp.where(d == 0, 0, jnp.where(d == 1, 1, 2)), h])
        return jnp.concatenate(tiles, axis=0)

    m_scr[...] = jnp.full((A_HEADS, TQ), M_INIT, F32)
    acc_scr[...] = jnp.zeros((A_HEADS, ACC_ROWS, TQ), F32)
    ones = jnp.ones((ACC_ROWS - A_HEAD_DIM, KEY_CHUNK), BF16)

    def stage_a(row0, mask, bias_of, s_buf, c_buf):
        for j in range(A_HEADS // 2):
            kc = jnp.concatenate([k_ref[pl.ds(row0, KEY_CHUNK), j * LANES:(j + 1) * LANES], mask], axis=1)
            s = lax.dot_general(kc, bq_scr[j], DN_LAST, preferred_element_type=F32)
            for hh in range(2):
                h = 2 * j + hh
                sh = s[:, hh * TQ:(hh + 1) * TQ]
                if bias_of is not None:
                    sh = sh + bias_of(h)
                s_buf[h] = sh
                c_buf[h:h + 1, :] = jnp.max(sh, axis=0, keepdims=True)

    def stage_a_far(ci, s_buf, c_buf):
        row0 = pl.multiple_of(ci * KEY_CHUNK, KEY_CHUNK)
        stage_a(row0, mb_scr[pl.ds(row0, KEY_CHUNK), :], None, s_buf, c_buf)

    def stage_b(blk0, s_buf, c_buf):
        for h in range(A_HEADS):
            m_old = m_scr[h:h + 1, :]
            m_new = jnp.maximum(m_old, c_buf[h:h + 1, :])
            p = jnp.exp2(s_buf[h] - m_new).astype(BF16)
            vt = [vt_ref[blk0 + t, h * A_HEAD_DIM:(h + 1) * A_HEAD_DIM, :] for t in range(blocks_per_chunk)]
            pv = jnp.dot(jnp.concatenate([jnp.concatenate(vt, axis=1), ones], axis=0), p,
                         preferred_element_type=F32)
            acc_scr[h] = jnp.exp2(m_old - m_new) * acc_scr[h] + pv
            m_scr[h:h + 1, :] = m_new

    stage_a(win0, mbw_scr[...], window_bias, sa_scr, ca_scr)

    def item_pair(t, carry):
        stage_a_far(2 * t, sb_scr, cb_scr)
        stage_b(jnp.where(t == 0, win_blk, (2 * t - 1) * blocks_per_chunk), sa_scr, ca_scr)

        @pl.when(2 * t + 1 < n_far)
        def _():
            stage_a_far(2 * t + 1, sa_scr, ca_scr)
            stage_b(2 * t * blocks_per_chunk, sb_scr, cb_scr)

        return carry

    lax.fori_loop(0, (n_far + 1) // 2, item_pair, 0)
    last_blk = jnp.where(n_far == 0, win_blk, (n_far - 1) * blocks_per_chunk)

    @pl.when(lax.rem(n_far, 2) == 1)
    def _():
        stage_b(last_blk, sb_scr, cb_scr)

    @pl.when(lax.rem(n_far, 2) == 0)
    def _():
        stage_b(last_blk, sa_scr, ca_scr)

    for h in range(A_HEADS):
        ot_scr[h * A_HEAD_DIM:(h + 1) * A_HEAD_DIM, :] = (
            acc_scr[h, 0:A_HEAD_DIM, :] / acc_scr[h, A_HEAD_DIM:A_HEAD_DIM + 1, :])

    out_ref[...] = ot_scr[...].T


def _t5_bucket(rel):
    half = REL_BUCKETS // 2
    max_exact = half // 2
    base = jnp.where(rel > 0, half, 0).astype(jnp.int32)
    n = jnp.abs(rel)
    nf = jnp.maximum(n, 1).astype(jnp.float32)
    large = max_exact + (jnp.log(nf / max_exact) / math.log(REL_MAX_DIST / max_exact)
                         * (half - max_exact)).astype(jnp.int32)
    large = jnp.minimum(large, half - 1)
    return base + jnp.where(n < max_exact, n, large)


def _attention(nat, wt, vt, rel_bias):
    bsz, seq, _ = nat.shape
    k_top = min(TOPK_MAX, seq // 4)
    idx_bits = max(1, int(math.ceil(math.log2(seq))))
    r = jnp.arange(TQ, dtype=jnp.int32)[:, None]
    t = jnp.arange(TQ, dtype=jnp.int32)[None, :]
    buckets = jnp.stack([_t5_bucket(r - t - TQ * d) for d in range(2)])
    return pl.pallas_call(
        functools.partial(_attn_kernel, k_top=k_top, idx_bits=idx_bits),
        grid=(bsz, seq // TQ),
        in_specs=[pl.BlockSpec(memory_space=pltpu.SMEM),
                  pl.BlockSpec((2, TQ, LANES), lambda b, i: (0, 0, 0)),
                  pl.BlockSpec((None, TQ, A_WIDTH), lambda b, i: (b, i, 0)),
                  pl.BlockSpec((None, TQ, A_WIDTH), lambda b, i: (b, i, 2)),
                  pl.BlockSpec((None, 16, TQ), lambda b, i: (b, 0, i)),
                  pl.BlockSpec((None, seq, LANES), lambda b, i: (b, 0, 3 * A_WIDTH // LANES)),
                  pl.BlockSpec((None, seq, A_WIDTH), lambda b, i: (b, 0, 1)),
                  pl.BlockSpec((None, seq // LANES, A_WIDTH, LANES), lambda b, i: (b, 0, 0, 0))],
        out_specs=pl.BlockSpec((None, TQ, A_WIDTH), lambda b, i: (b, i, 0)),
        out_shape=jax.ShapeDtypeStruct((bsz, seq, A_WIDTH), F32),
        scratch_shapes=[pltpu.VMEM((seq, LANES), I32),
                        pltpu.VMEM((32, seq // 32, LANES), U32),
                        pltpu.VMEM((seq, LANES), BF16),
                        pltpu.VMEM((KEY_CHUNK, LANES), BF16),
                        pltpu.VMEM((3, A_HEADS, TQ, LANES), F32),
                        pltpu.VMEM((A_HEADS // 2, 2 * TQ, LANES), BF16),
                        pltpu.VMEM((A_HEADS // 2, 2 * TQ, 2 * LANES), BF16),
                        pltpu.VMEM((A_HEADS, KEY_CHUNK, TQ), F32),
                        pltpu.VMEM((A_HEADS, KEY_CHUNK, TQ), F32),
                        pltpu.VMEM((A_HEADS, TQ), F32),
                        pltpu.VMEM((A_HEADS, TQ), F32),
                        pltpu.VMEM((A_HEADS, TQ), F32),
                        pltpu.VMEM((A_HEADS, ACC_ROWS, TQ), F32),
                        pltpu.VMEM((A_WIDTH, TQ), F32)],
        compiler_params=pltpu.CompilerParams(dimension_semantics=("arbitrary", "arbitrary"),
                                             vmem_limit_bytes=VMEM_LIMIT),
        name="sparse_attn",
    )(rel_bias, buckets, nat, nat, wt, nat, nat, vt)


def _out_kernel(attn_ref, za_ref, cb_ref, ga_ref, gb_ref, x_ref, p_ref, wa_ref, wb_ref, wo_ref,
                pg_ref, wpg_ref, wpp_ref, fg_ref, out_ref):
    za = za_ref[...]
    ya_in = (attn_ref[...] * (za * _sigmoid(za))).astype(BF16)
    y_a = jnp.dot(ya_in, wa_ref[...], preferred_element_type=F32)
    y_b = jnp.dot(cb_ref[...], wb_ref[...], preferred_element_type=F32)
    merged = _sigmoid(ga_ref[...]) * y_a + _sigmoid(gb_ref[...]) * y_b
    x1 = x_ref[...] + jnp.dot(merged.astype(BF16), wo_ref[...], preferred_element_type=F32)
    e = jnp.dot(p_ref[...].astype(BF16), wpp_ref[...], preferred_element_type=F32)
    gate = _sigmoid(jnp.dot(_rms(x1, pg_ref[...]).astype(BF16), wpg_ref[...], preferred_element_type=F32))
    out_ref[...] = _rms(x1 + gate * e, fg_ref[...])


def _out_stage(attn, ew, cbr, x, p, wa, wb, wo, pg, wpg, wpp, fg, tm):
    bsz, seq, d = x.shape
    full = lambda a: pl.BlockSpec(a.shape, lambda b, s: (0,) * a.ndim)
    return pl.pallas_call(
        _out_kernel,
        grid=(bsz, seq // tm),
        in_specs=[pl.BlockSpec((None, tm, A_WIDTH), lambda b, s: (b, s, 0)),
                  pl.BlockSpec((None, tm, A_WIDTH), lambda b, s: (b, s, 2 * D_MODEL // A_WIDTH)),
                  pl.BlockSpec((None, tm, B_WIDTH), lambda b, s: (b, s, 0)),
                  pl.BlockSpec((None, tm, D_MODEL), lambda b, s: (b, s, 0)),
                  pl.BlockSpec((None, tm, D_MODEL), lambda b, s: (b, s, 1)),
                  pl.BlockSpec((None, tm, d), lambda b, s: (b, s, 0)),
                  pl.BlockSpec((None, tm, PLE_DIM), lambda b, s: (b, s, 0)),
                  full(wa), full(wb), full(wo), full(pg), full(wpg), full(wpp), full(fg)],
        out_specs=pl.BlockSpec((None, tm, d), lambda b, s: (b, s, 0)),
        out_shape=jax.ShapeDtypeStruct((bsz, seq, d), F32),
        compiler_params=pltpu.CompilerParams(dimension_semantics=("arbitrary", "arbitrary"),
                                             vmem_limit_bytes=VMEM_LIMIT),
        name="merge_out",
    )(attn, ew, cbr, ew, ew, x, p, wa, wb, wo, pg, wpg, wpp, fg)


def kernel(x, p, norm_in_g, w_in, conv_w, conv_b, conv_ln_g, conv_ln_b, w_branch_a, w_branch_b, w_out,
           ple_norm_g, w_ple_gate, w_ple_proj, rel_bias, final_norm_g):
    seq = x.shape[1]
    assert w_in.shape[0] == 1, "one layer: the fused output stage ends with the final norm"
    assert seq % KEY_CHUNK == 0 and seq >= 2 * TQ and x.shape[2] == D_MODEL
    offs = np.cumsum((0,) + IN_SIZES)
    row = lambda v: v.reshape(1, -1)
    for i in range(1):
        w = w_in[i]
        wq, wk, wv, wza, wqi, wki, wwi, wglu, wzb, wga, wgb = [w[:, offs[n]:offs[n + 1]] for n in range(11)]
        wnat = jnp.concatenate([wq, wk, wqi, wki, wki], axis=1).astype(BF16)
        wew = jnp.concatenate([wga, wgb, wza], axis=1).astype(BF16)
        wvt = wv.T.astype(BF16)
        wwt = jnp.pad(wwi.T, ((0, 16 - IDX_HEADS), (0, 0))).astype(BF16)
        nat, ew, vt, wt, cbr = _proj(x, row(norm_in_g[i]), wnat, wglu.astype(BF16), wzb.astype(BF16), wew, wvt,
                                     wwt, jnp.repeat(conv_w[i].reshape(CONV_K, B_WIDTH), SUBLANES, axis=0),
                                     row(conv_b[i]),
                                     row(conv_ln_g[i]), row(conv_ln_b[i]), tm=256)
        attn = _attention(nat, wt, vt, rel_bias)
        x = _out_stage(attn, ew, cbr, x, p[i], w_branch_a[i].astype(BF16), w_branch_b[i].astype(BF16),
                       w_out[i].astype(BF16), row(ple_norm_g[i]), w_ple_gate[i].astype(BF16),
                       w_ple_proj[i].astype(BF16), row(final_norm_g), tm=512)
    return x
```

```python
import functools
import math

import numpy as np
import jax
import jax.numpy as jnp
from jax import lax
from jax.experimental import pallas as pl
from jax.experimental.pallas import tpu as pltpu

D_MODEL = 1024
CHUNK = 64
CHUNK_SHIFT = 6
A_HEADS = 8
A_HEAD_DIM = 64
A_WIDTH = A_HEADS * A_HEAD_DIM
IDX_HEADS = 8
IDX_DIM = 64
TOPK_MAX = 256
REL_BUCKETS = 32
REL_MAX_DIST = 128
B_WIDTH = 512
CONV_K = 31
PLE_DIM = 256
EPS = 1e-6
IN_SIZES = (A_WIDTH, A_WIDTH, A_WIDTH, A_WIDTH, IDX_HEADS * IDX_DIM, IDX_DIM, IDX_HEADS,
            2 * B_WIDTH, B_WIDTH, D_MODEL, D_MODEL)

LANES = 128
SUBLANES = 8
TQ = 128
KEY_CHUNK = 512
ACC_ROWS = A_HEAD_DIM + 16
FAR_BUCKET = 15
CONV_HALO = 32
VMEM_LIMIT = 48 * 1024 * 1024

INT_MIN = -2 ** 31
INT_MAX = 2 ** 31 - 1
KEY_NEG_INF = (0xFF800000 ^ 0x7FFFFFFF) - 2 ** 32
KEY_POS_INF = 0x7F800000
GROUP_ROWS = 32 * SUBLANES
M_INIT = -3.0e38
MASKED = -1.0e30
LOG2E = math.log2(math.e)

F32 = jnp.float32
BF16 = jnp.bfloat16
I32 = jnp.int32
U32 = jnp.uint32
DN_LAST = (((1,), (1,)), ((), ()))


def _rms(x, g):
    return x * lax.rsqrt(jnp.mean(x * x, axis=-1, keepdims=True) + EPS) * g


def _sigmoid(x):
    return 1.0 / (1.0 + jnp.exp(-x))


def _proj_kernel(x_ref, g_ref, wnat_ref, wglu_ref, wzb_ref, wew_ref, wvt_ref, wwt_ref, cw_ref, cb_ref, lg_ref,
                 lb_ref, nat_ref, ew_ref, vt_ref, wt_ref, cbr_ref, ext_ref, *, row_tile):
    s = pl.program_id(1)
    tm = x_ref.shape[0]
    h = _rms(x_ref[...], g_ref[...]).astype(BF16)

    @pl.when(s == 0)
    def _():
        ext_ref[0:CONV_HALO, :] = jnp.zeros((CONV_HALO, B_WIDTH), F32)
        ext_ref[CONV_HALO + tm:, :] = jnp.zeros((SUBLANES, B_WIDTH), F32)

    @pl.when(s > 0)
    def _():
        ext_ref[0:CONV_HALO, :] = ext_ref[tm:tm + CONV_HALO, :]

    glu = jnp.dot(h, wglu_ref[...], preferred_element_type=F32)
    ext_ref[CONV_HALO:CONV_HALO + tm, :] = glu[:, :B_WIDTH] * _sigmoid(glu[:, B_WIDTH:])
    zb_all = jnp.dot(h, wzb_ref[...], preferred_element_type=F32)
    first = CONV_HALO - (CONV_K - 1)
    for r in range(tm // row_tile):
        r0 = r * row_tile
        acc = jnp.zeros((row_tile, B_WIDTH), F32)
        for res in range(SUBLANES):
            part = jnp.zeros((row_tile + SUBLANES, B_WIDTH), F32)
            for off in range(res, first + CONV_K, SUBLANES):
                if off >= first:
                    tap = cw_ref[(off - first) * SUBLANES:(off - first + 1) * SUBLANES, :]
                    part = part + (ext_ref[r0 + off - res:r0 + off - res + row_tile + SUBLANES, :]
                                   * jnp.concatenate([tap] * (row_tile // SUBLANES + 1), axis=0))
            acc = acc + part[res:res + row_tile]
        acc = acc + cb_ref[...]
        mu = jnp.mean(acc, axis=-1, keepdims=True)
        xc = acc - mu
        var = jnp.mean(xc * xc, axis=-1, keepdims=True)
        y = xc * lax.rsqrt(var + EPS) * lg_ref[...] + lb_ref[...]
        zb = zb_all[r0:r0 + row_tile]
        cbr_ref[r0:r0 + row_tile, :] = ((y * _sigmoid(y)) * (zb * _sigmoid(zb))).astype(BF16)

    nat_ref[...] = jnp.dot(h, wnat_ref[...], preferred_element_type=F32).astype(BF16)
    ew_ref[...] = jnp.dot(h, wew_ref[...], preferred_element_type=F32)
    vt = lax.dot_general(wvt_ref[...], h, DN_LAST, preferred_element_type=F32).astype(BF16)
    for c in range(vt_ref.shape[0]):
        vt_ref[c] = vt[:, c * LANES:(c + 1) * LANES]
    wt_ref[...] = lax.dot_general(wwt_ref[...], h, DN_LAST, preferred_element_type=F32)


def _proj(x, g, wnat, wglu, wzb, wew, wvt, wwt, cw, cb, lg, lb, tm):
    bsz, seq, d = x.shape
    n_nat, n_ew = wnat.shape[1], wew.shape[1]
    full = lambda a: pl.BlockSpec(a.shape, lambda b, s: (0,) * a.ndim)
    return pl.pallas_call(
        functools.partial(_proj_kernel, row_tile=64),
        grid=(bsz, seq // tm),
        in_specs=[pl.BlockSpec((None, tm, d), lambda b, s: (b, s, 0)),
                  full(g), full(wnat), full(wglu), full(wzb), full(wew), full(wvt), full(wwt),
                  full(cw), full(cb), full(lg), full(lb)],
        out_specs=[pl.BlockSpec((None, tm, n_nat), lambda b, s: (b, s, 0)),
                   pl.BlockSpec((None, tm, n_ew), lambda b, s: (b, s, 0)),
                   pl.BlockSpec((None, tm // LANES, A_WIDTH, LANES), lambda b, s: (b, s, 0, 0)),
                   pl.BlockSpec((None, 16, tm), lambda b, s: (b, 0, s)),
                   pl.BlockSpec((None, tm, B_WIDTH), lambda b, s: (b, s, 0))],
        out_shape=[jax.ShapeDtypeStruct((bsz, seq, n_nat), BF16),
                   jax.ShapeDtypeStruct((bsz, seq, n_ew), F32),
                   jax.ShapeDtypeStruct((bsz, seq // LANES, A_WIDTH, LANES), BF16),
                   jax.ShapeDtypeStruct((bsz, 16, seq), F32),
                   jax.ShapeDtypeStruct((bsz, seq, B_WIDTH), BF16)],
        scratch_shapes=[pltpu.VMEM((tm + CONV_HALO + SUBLANES, B_WIDTH), F32)],
        compiler_params=pltpu.CompilerParams(dimension_semantics=("arbitrary", "arbitrary"),
                                             vmem_limit_bytes=VMEM_LIMIT),
        name="in_proj_conv",
    )(x, g, wnat, wglu, wzb, wew, wvt, wwt, cw, cb, lg, lb)


def _bit_planes(words):
    a = list(words)
    j, m = 16, 0x0000FFFF
    while j:
        k = 0
        while k < 32:
            t = (a[k] ^ (a[k + j] >> np.uint32(j))) & np.uint32(m)
            a[k] = a[k] ^ t
            a[k + j] = a[k + j] ^ (t << np.uint32(j))
            k = (k + j + 1) & ~j
        j >>= 1
        m ^= (m << j) & 0xFFFFFFFF
    return a


def _attn_kernel(relb_ref, bucket_ref, q_ref, qi_ref, wt_ref, kidx_ref, k_ref, vt_ref, out_ref,
                 key_scr, plane_scr, mb_scr, mbw_scr, bias_scr, bidx_scr, bq_scr, sa_scr, sb_scr,
                 ca_scr, cb_scr, m_scr, acc_scr, ot_scr, *, k_top, idx_bits):
    i = pl.program_id(1)
    n_chunks = i // (KEY_CHUNK // TQ) + 1
    idx_scale = (IDX_DIM ** -0.5) * (IDX_HEADS ** -0.5)
    attn_scale = A_HEAD_DIM ** -0.5

    @pl.when(i == 0)
    def _init_bias():
        plane_scr[...] = jnp.zeros(plane_scr.shape, U32)
        for h in range(A_HEADS):
            bias_scr[2, h] = jnp.zeros((TQ, LANES), F32)
        for d in range(2):
            for h in range(A_HEADS):
                bias_scr[d, h] = jnp.zeros((TQ, LANES), F32) + relb_ref[0, h]

        def fill(b, carry):
            for d in range(2):
                hit = bucket_ref[d] == b
                for h in range(A_HEADS):
                    bias_scr[d, h] = jnp.where(hit, relb_ref[b, h], bias_scr[d, h])
            return carry

        lax.fori_loop(1, REL_BUCKETS, fill, 0)
        for d in range(2):
            for h in range(A_HEADS):
                bias_scr[d, h] = (bias_scr[d, h] - relb_ref[FAR_BUCKET, h]) * LOG2E
        eye = (lax.broadcasted_iota(I32, (2 * TQ, LANES), 0) & (TQ - 1)) == lax.broadcasted_iota(
            I32, (2 * TQ, LANES), 1)
        for j in range(A_HEADS // 2):
            bq_scr[j, :, LANES:] = jnp.where(eye, 1.0, 0.0).astype(BF16)

    lane = lax.broadcasted_iota(I32, (TQ, LANES), 1)
    low = lane < A_HEAD_DIM
    for j in range(A_HEADS // 2):
        qi = qi_ref[:, j * LANES:(j + 1) * LANES].astype(F32)
        bidx_scr[j, 0:TQ, :] = jnp.where(low, qi, 0.0).astype(BF16)
        bidx_scr[j, TQ:2 * TQ, :] = jnp.where(low, 0.0, qi).astype(BF16)
        qq = q_ref[:, j * LANES:(j + 1) * LANES].astype(F32) * (attn_scale * LOG2E)
        bq_scr[j, 0:TQ, 0:LANES] = jnp.where(low, qq, 0.0).astype(BF16)
        bq_scr[j, TQ:2 * TQ, 0:LANES] = jnp.where(low, 0.0, qq).astype(BF16)

    q_chunk = (i * TQ + lax.broadcasted_iota(I32, (1, LANES), 1)) >> CHUNK_SHIFT
    w_scaled = [wt_ref[h:h + 1, :] * idx_scale for h in range(IDX_HEADS)]
    row_iota = lax.broadcasted_iota(I32, (KEY_CHUNK, LANES), 0)

    def score_rows(base, rows, masked):
        kc = kidx_ref[pl.ds(base, rows), :]
        acc = jnp.zeros((rows, LANES), F32)
        for j in range(IDX_HEADS // 2):
            d = lax.dot_general(kc, bidx_scr[j], DN_LAST, preferred_element_type=F32)
            acc = acc + jnp.maximum(d[:, :TQ], 0.0) * w_scaled[2 * j]
            acc = acc + jnp.maximum(d[:, TQ:], 0.0) * w_scaled[2 * j + 1]
        if masked:
            acc = jnp.where(((base + row_iota[0:rows]) >> CHUNK_SHIFT) <= q_chunk, acc, -jnp.inf)
        bits = lax.bitcast_convert_type(acc, I32)
        ukey = lax.bitcast_convert_type(bits ^ ((bits >> 31) | INT_MIN), U32)
        key_scr[pl.ds(base, rows), :] = lax.bitcast_convert_type(ukey ^ np.uint32(0x80000000), I32)
        for gg in range(rows // GROUP_ROWS):
            words = [ukey[gg * GROUP_ROWS + w * SUBLANES:gg * GROUP_ROWS + (w + 1) * SUBLANES] for w in range(32)]
            grp = pl.ds(pl.multiple_of(base // 32 + gg * SUBLANES, SUBLANES), SUBLANES)
            for b, plane in enumerate(_bit_planes(words)):
                plane_scr[b, grp, :] = plane

    def score_chunk(ci):
        score_rows(pl.multiple_of(ci * KEY_CHUNK, KEY_CHUNK), KEY_CHUNK, masked=False)

    def score_pair(t, carry):
        score_chunk(2 * t)
        score_chunk(2 * t + 1)
        return carry

    lax.fori_loop(0, (n_chunks - 1) // 2, score_pair, 0)

    @pl.when(lax.rem(n_chunks - 1, 2) == 1)
    def _():
        score_chunk(n_chunks - 2)

    last0 = pl.multiple_of((n_chunks - 1) * KEY_CHUNK, KEY_CHUNK)
    half = KEY_CHUNK // 2
    second_half = lax.rem(i, KEY_CHUNK // TQ) >= half // TQ

    @pl.when(second_half)
    def _():
        score_rows(last0, KEY_CHUNK, masked=True)

    @pl.when(jnp.logical_not(second_half))
    def _():
        score_rows(last0, half, masked=True)
        key_scr[pl.ds(pl.multiple_of(last0 + half, half), half), :] = jnp.full((half, LANES), KEY_NEG_INF, I32)

    n_groups = (n_chunks - 1) * (KEY_CHUNK // GROUP_ROWS) + jnp.where(second_half, 2, 1) * (half // GROUP_ROWS)
    max_groups = plane_scr.shape[1] // SUBLANES

    def lane_total(words):
        pc = [lax.population_count(w) for w in words]
        while len(pc) > 1:
            pc = [pc[k] + pc[k + 1] for k in range(0, len(pc) - 1, 2)] + ([pc[-1]] if len(pc) % 2 else [])
        return jnp.sum(lax.bitcast_convert_type(pc[0], I32), axis=0, keepdims=True)

    def choose_bit(b, carry):
        alive, need, prefix = carry
        planes = plane_scr[b]
        ones = [alive[g] & planes[g * SUBLANES:(g + 1) * SUBLANES] for g in range(max_groups)]
        cnt = lane_total(ones)
        take = cnt >= need
        alive = tuple(jnp.where(take, ones[g], alive[g] ^ ones[g]) for g in range(max_groups))
        bit = lax.shift_left(jnp.uint32(1), jnp.asarray(31 - b, U32))
        return alive, jnp.where(take, need, need - cnt), jnp.where(take, prefix | bit, prefix)

    alive0 = tuple(jnp.zeros((SUBLANES, LANES), U32) + jnp.where(g < n_groups, np.uint32(0xFFFFFFFF), np.uint32(0))
                   for g in range(max_groups))
    alive, need_left, prefix = lax.fori_loop(
        0, 32, choose_bit, (alive0, jnp.full((1, LANES), k_top, I32), jnp.zeros((1, LANES), U32)))
    thr = lax.bitcast_convert_type(prefix ^ np.uint32(0x80000000), I32)
    n_tied = lane_total(list(alive))

    tie = (n_tied > need_left) & (thr > KEY_NEG_INF) & (thr < KEY_POS_INF)
    any_tie = jnp.max(tie.astype(I32)) > 0

    def write_mask(select):
        def body(ci, carry):
            base = pl.multiple_of(ci * KEY_CHUNK, KEY_CHUNK)
            blk = key_scr[pl.ds(base, KEY_CHUNK), :]
            sel = select(blk, base) & (blk > KEY_NEG_INF) & (blk < KEY_POS_INF)
            mb_scr[pl.ds(base, KEY_CHUNK), :] = jnp.where(sel, 0.0, MASKED).astype(BF16)
            return carry
        lax.fori_loop(0, n_chunks, body, 0)

    @pl.when(jnp.logical_not(any_tie))
    def _():
        write_mask(lambda blk, base: blk >= thr)

    @pl.when(any_tie)
    def _break_ties():
        sub = lax.broadcasted_iota(I32, (SUBLANES, LANES), 0)
        all_ones = np.uint32(0xFFFFFFFF)

        def tied_below(row):
            parts = []
            for g in range(max_groups):
                r = row - g * GROUP_ROWS
                words = jnp.clip(r >> 3, 0, 32)
                top = ~lax.shift_right_logical(jnp.zeros((1, LANES), U32) + all_ones,
                                               jnp.minimum(words, 31).astype(U32))
                top = jnp.where(words >= 32, all_ones, top)
                edge_bit = lax.shift_left(jnp.zeros((1, LANES), U32) + np.uint32(1),
                                          (31 - jnp.minimum(words, 31)).astype(U32))
                edge = (sub < (r & (SUBLANES - 1))) & (r >= 0) & (words < 32)
                parts.append(alive[g] & (top | jnp.where(edge, edge_bit, np.uint32(0))))
            return lane_total(parts)

        def grow(p, last):
            trial = last + lax.shift_left(jnp.int32(1), idx_bits - 1 - p)
            return jnp.where(tied_below(trial) <= need_left - 1, trial, last)

        last = lax.fori_loop(0, idx_bits, grow, jnp.zeros((1, LANES), I32))
        last = jnp.where(tie, last, INT_MAX)
        write_mask(lambda blk, base: (blk > thr) | ((blk == thr) & ((base + row_iota) <= last)))

    blocks_per_chunk = KEY_CHUNK // TQ
    win_blk = jnp.maximum(i - (blocks_per_chunk - 1), 0)
    win0 = pl.multiple_of(win_blk * TQ, TQ)
    mbw_scr[...] = mb_scr[pl.ds(win0, KEY_CHUNK), :]
    mb_scr[pl.ds(win0, KEY_CHUNK), :] = jnp.full((KEY_CHUNK, LANES), MASKED, BF16)
    n_far = (win_blk + blocks_per_chunk - 1) // blocks_per_chunk

    def window_bias(h):
        tiles = []
        for t in range(blocks_per_chunk):
            d = i - (win_blk + t)
            tiles.append(bias_scr[jnp.where(d == 0, 0, jnp.where(d == 1, 1, 2)), h])
        return jnp.concatenate(tiles, axis=0)

    m_scr[...] = jnp.full((A_HEADS, TQ), M_INIT, F32)
    acc_scr[...] = jnp.zeros((A_HEADS, ACC_ROWS, TQ), F32)
    ones = jnp.ones((ACC_ROWS - A_HEAD_DIM, KEY_CHUNK), BF16)

    def stage_a(row0, mask, bias_of, s_buf, c_buf):
        for j in range(A_HEADS // 2):
            kc = jnp.concatenate([k_ref[pl.ds(row0, KEY_CHUNK), j * LANES:(j + 1) * LANES], mask], axis=1)
            s = lax.dot_general(kc, bq_scr[j], DN_LAST, preferred_element_type=F32)
            for hh in range(2):
                h = 2 * j + hh
                sh = s[:, hh * TQ:(hh + 1) * TQ]
                if bias_of is not None:
                    sh = sh + bias_of(h)
                s_buf[h] = sh
                c_buf[h:h + 1, :] = jnp.max(sh, axis=0, keepdims=True)

    def stage_a_far(ci, s_buf, c_buf):
        row0 = pl.multiple_of(ci * KEY_CHUNK, KEY_CHUNK)
        stage_a(row0, mb_scr[pl.ds(row0, KEY_CHUNK), :], None, s_buf, c_buf)

    def stage_b(blk0, s_buf, c_buf):
        for h in range(A_HEADS):
            m_old = m_scr[h:h + 1, :]
            m_new = jnp.maximum(m_old, c_buf[h:h + 1, :])
            p = jnp.exp2(s_buf[h] - m_new).astype(BF16)
            vt = [vt_ref[blk0 + t, h * A_HEAD_DIM:(h + 1) * A_HEAD_DIM, :] for t in range(blocks_per_chunk)]
            pv = jnp.dot(jnp.concatenate([jnp.concatenate(vt, axis=1), ones], axis=0), p,
                         preferred_element_type=F32)
            acc_scr[h] = jnp.exp2(m_old - m_new) * acc_scr[h] + pv
            m_scr[h:h + 1, :] = m_new

    stage_a(win0, mbw_scr[...], window_bias, sa_scr, ca_scr)

    def item_pair(t, carry):
        stage_a_far(2 * t, sb_scr, cb_scr)
        stage_b(jnp.where(t == 0, win_blk, (2 * t - 1) * blocks_per_chunk), sa_scr, ca_scr)

        @pl.when(2 * t + 1 < n_far)
        def _():
            stage_a_far(2 * t + 1, sa_scr, ca_scr)
            stage_b(2 * t * blocks_per_chunk, sb_scr, cb_scr)

        return carry

    lax.fori_loop(0, (n_far + 1) // 2, item_pair, 0)
    last_blk = jnp.where(n_far == 0, win_blk, (n_far - 1) * blocks_per_chunk)

    @pl.when(lax.rem(n_far, 2) == 1)
    def _():
        stage_b(last_blk, sb_scr, cb_scr)

    @pl.when(lax.rem(n_far, 2) == 0)
    def _():
        stage_b(last_blk, sa_scr, ca_scr)

    for h in range(A_HEADS):
        ot_scr[h * A_HEAD_DIM:(h + 1) * A_HEAD_DIM, :] = (
            acc_scr[h, 0:A_HEAD_DIM, :] / acc_scr[h, A_HEAD_DIM:A_HEAD_DIM + 1, :])

    out_ref[...] = ot_scr[...].T


def _t5_bucket(rel):
    half = REL_BUCKETS // 2
    max_exact = half // 2
    base = jnp.where(rel > 0, half, 0).astype(jnp.int32)
    n = jnp.abs(rel)
    nf = jnp.maximum(n, 1).astype(jnp.float32)
    large = max_exact + (jnp.log(nf / max_exact) / math.log(REL_MAX_DIST / max_exact)
                         * (half - max_exact)).astype(jnp.int32)
    large = jnp.minimum(large, half - 1)
    return base + jnp.where(n < max_exact, n, large)


def _attention(nat, wt, vt, rel_bias):
    bsz, seq, _ = nat.shape
    k_top = min(TOPK_MAX, seq // 4)
    idx_bits = max(1, int(math.ceil(math.log2(seq))))
    r = jnp.arange(TQ, dtype=jnp.int32)[:, None]
    t = jnp.arange(TQ, dtype=jnp.int32)[None, :]
    buckets = jnp.stack([_t5_bucket(r - t - TQ * d) for d in range(2)])
    return pl.pallas_call(
        functools.partial(_attn_kernel, k_top=k_top, idx_bits=idx_bits),
        grid=(bsz, seq // TQ),
        in_specs=[pl.BlockSpec(memory_space=pltpu.SMEM),
                  pl.BlockSpec((2, TQ, LANES), lambda b, i: (0, 0, 0)),
                  pl.BlockSpec((None, TQ, A_WIDTH), lambda b, i: (b, i, 0)),
                  pl.BlockSpec((None, TQ, A_WIDTH), lambda b, i: (b, i, 2)),
                  pl.BlockSpec((None, 16, TQ), lambda b, i: (b, 0, i)),
                  pl.BlockSpec((None, seq, LANES), lambda b, i: (b, 0, 3 * A_WIDTH // LANES)),
                  pl.BlockSpec((None, seq, A_WIDTH), lambda b, i: (b, 0, 1)),
                  pl.BlockSpec((None, seq // LANES, A_WIDTH, LANES), lambda b, i: (b, 0, 0, 0))],
        out_specs=pl.BlockSpec((None, TQ, A_WIDTH), lambda b, i: (b, i, 0)),
        out_shape=jax.ShapeDtypeStruct((bsz, seq, A_WIDTH), F32),
        scratch_shapes=[pltpu.VMEM((seq, LANES), I32),
                        pltpu.VMEM((32, seq // 32, LANES), U32),
                        pltpu.VMEM((seq, LANES), BF16),
                        pltpu.VMEM((KEY_CHUNK, LANES), BF16),
                        pltpu.VMEM((3, A_HEADS, TQ, LANES), F32),
                        pltpu.VMEM((A_HEADS // 2, 2 * TQ, LANES), BF16),
                        pltpu.VMEM((A_HEADS // 2, 2 * TQ, 2 * LANES), BF16),
                        pltpu.VMEM((A_HEADS, KEY_CHUNK, TQ), F32),
                        pltpu.VMEM((A_HEADS, KEY_CHUNK, TQ), F32),
                        pltpu.VMEM((A_HEADS, TQ), F32),
                        pltpu.VMEM((A_HEADS, TQ), F32),
                        pltpu.VMEM((A_HEADS, TQ), F32),
                        pltpu.VMEM((A_HEADS, ACC_ROWS, TQ), F32),
                        pltpu.VMEM((A_WIDTH, TQ), F32)],
        compiler_params=pltpu.CompilerParams(dimension_semantics=("arbitrary", "arbitrary"),
                                             vmem_limit_bytes=VMEM_LIMIT),
        name="sparse_attn",
    )(rel_bias, buckets, nat, nat, wt, nat, nat, vt)


def _out_kernel(attn_ref, za_ref, cb_ref, ga_ref, gb_ref, x_ref, p_ref, wa_ref, wb_ref, wo_ref,
                pg_ref, wpg_ref, wpp_ref, fg_ref, out_ref):
    za = za_ref[...]
    ya_in = (attn_ref[...] * (za * _sigmoid(za))).astype(BF16)
    y_a = jnp.dot(ya_in, wa_ref[...], preferred_element_type=F32)
    y_b = jnp.dot(cb_ref[...], wb_ref[...], preferred_element_type=F32)
    merged = _sigmoid(ga_ref[...]) * y_a + _sigmoid(gb_ref[...]) * y_b
    x1 = x_ref[...] + jnp.dot(merged.astype(BF16), wo_ref[...], preferred_element_type=F32)
    e = jnp.dot(p_ref[...].astype(BF16), wpp_ref[...], preferred_element_type=F32)
    gate = _sigmoid(jnp.dot(_rms(x1, pg_ref[...]).astype(BF16), wpg_ref[...], preferred_element_type=F32))
    out_ref[...] = _rms(x1 + gate * e, fg_ref[...])


def _out_stage(attn, ew, cbr, x, p, wa, wb, wo, pg, wpg, wpp, fg, tm):
    bsz, seq, d = x.shape
    full = lambda a: pl.BlockSpec(a.shape, lambda b, s: (0,) * a.ndim)
    return pl.pallas_call(
        _out_kernel,
        grid=(bsz, seq // tm),
        in_specs=[pl.BlockSpec((None, tm, A_WIDTH), lambda b, s: (b, s, 0)),
                  pl.BlockSpec((None, tm, A_WIDTH), lambda b, s: (b, s, 2 * D_MODEL // A_WIDTH)),
                  pl.BlockSpec((None, tm, B_WIDTH), lambda b, s: (b, s, 0)),
                  pl.BlockSpec((None, tm, D_MODEL), lambda b, s: (b, s, 0)),
                  pl.BlockSpec((None, tm, D_MODEL), lambda b, s: (b, s, 1)),
                  pl.BlockSpec((None, tm, d), lambda b, s: (b, s, 0)),
                  pl.BlockSpec((None, tm, PLE_DIM), lambda b, s: (b, s, 0)),
                  full(wa), full(wb), full(wo), full(pg), full(wpg), full(wpp), full(fg)],
        out_specs=pl.BlockSpec((None, tm, d), lambda b, s: (b, s, 0)),
        out_shape=jax.ShapeDtypeStruct((bsz, seq, d), F32),
        compiler_params=pltpu.CompilerParams(dimension_semantics=("arbitrary", "arbitrary"),
                                             vmem_limit_bytes=VMEM_LIMIT),
        name="merge_out",
    )(attn, ew, cbr, ew, ew, x, p, wa, wb, wo, pg, wpg, wpp, fg)


def kernel(x, p, norm_in_g, w_in, conv_w, conv_b, conv_ln_g, conv_ln_b, w_branch_a, w_branch_b, w_out,
           ple_norm_g, w_ple_gate, w_ple_proj, rel_bias, final_norm_g):
    seq = x.shape[1]
    assert w_in.shape[0] == 1, "one layer: the fused output stage ends with the final norm"
    assert seq % KEY_CHUNK == 0 and seq >= 2 * TQ and x.shape[2] == D_MODEL
    offs = np.cumsum((0,) + IN_SIZES)
    row = lambda v: v.reshape(1, -1)
    for i in range(1):
        w = w_in[i]
        wq, wk, wv, wza, wqi, wki, wwi, wglu, wzb, wga, wgb = [w[:, offs[n]:offs[n + 1]] for n in range(11)]
        wnat = jnp.concatenate([wq, wk, wqi, wki, wki], axis=1).astype(BF16)
        wew = jnp.concatenate([wga, wgb, wza], axis=1).astype(BF16)
        wvt = wv.T.astype(BF16)
        wwt = jnp.pad(wwi.T, ((0, 16 - IDX_HEADS), (0, 0))).astype(BF16)
        nat, ew, vt, wt, cbr = _proj(x, row(norm_in_g[i]), wnat, wglu.astype(BF16), wzb.astype(BF16), wew, wvt,
                                     wwt, jnp.repeat(conv_w[i].reshape(CONV_K, B_WIDTH), SUBLANES, axis=0),
                                     row(conv_b[i]),
                                     row(conv_ln_g[i]), row(conv_ln_b[i]), tm=256)
        attn = _attention(nat, wt, vt, rel_bias)
        x = _out_stage(attn, ew, cbr, x, p[i], w_branch_a[i].astype(BF16), w_branch_b[i].astype(BF16),
                       w_out[i].astype(BF16), row(ple_norm_g[i]), w_ple_gate[i].astype(BF16),
                       w_ple_proj[i].astype(BF16), row(final_norm_g), tm=512)
    return x
```

```python
import functools
import math

import numpy as np
import jax
import jax.numpy as jnp
from jax import lax
from jax.experimental import pallas as pl
from jax.experimental.pallas import tpu as pltpu

D_MODEL = 1024
CHUNK = 64
CHUNK_SHIFT = 6
A_HEADS = 8
A_HEAD_DIM = 64
A_WIDTH = A_HEADS * A_HEAD_DIM
IDX_HEADS = 8
IDX_DIM = 64
TOPK_MAX = 256
REL_BUCKETS = 32
REL_MAX_DIST = 128
B_WIDTH = 512
CONV_K = 31
PLE_DIM = 256
EPS = 1e-6
IN_SIZES = (A_WIDTH, A_WIDTH, A_WIDTH, A_WIDTH, IDX_HEADS * IDX_DIM, IDX_DIM, IDX_HEADS,
            2 * B_WIDTH, B_WIDTH, D_MODEL, D_MODEL)

LANES = 128
SUBLANES = 8
TQ = 128
KEY_CHUNK = 512
ACC_ROWS = A_HEAD_DIM + 16
FAR_BUCKET = 15
CONV_HALO = 32
VMEM_LIMIT = 48 * 1024 * 1024

INT_MIN = -2 ** 31
INT_MAX = 2 ** 31 - 1
KEY_NEG_INF = (0xFF800000 ^ 0x7FFFFFFF) - 2 ** 32
KEY_POS_INF = 0x7F800000
GROUP_ROWS = 32 * SUBLANES
M_INIT = -3.38e38
MASKED = -3.0e38
LOG2E = math.log2(math.e)

F32 = jnp.float32
BF16 = jnp.bfloat16
I32 = jnp.int32
U32 = jnp.uint32
DN_LAST = (((1,), (1,)), ((), ()))


def _rms(x, g):
    return x * lax.rsqrt(jnp.mean(x * x, axis=-1, keepdims=True) + EPS) * g


def _sigmoid(x):
    return 1.0 / (1.0 + jnp.exp(-x))


def _proj_kernel(x_ref, g_ref, wnat_ref, wglu_ref, wzb_ref, wew_ref, wvt_ref, wwt_ref, cw_ref, cb_ref, lg_ref,
                 lb_ref, nat_ref, ew_ref, vt_ref, wt_ref, cbr_ref, ext_ref, *, row_tile):
    s = pl.program_id(1)
    tm = x_ref.shape[0]
    h = _rms(x_ref[...], g_ref[...]).astype(BF16)

    @pl.when(s == 0)
    def _():
        ext_ref[0:CONV_HALO, :] = jnp.zeros((CONV_HALO, B_WIDTH), F32)
        ext_ref[CONV_HALO + tm:, :] = jnp.zeros((SUBLANES, B_WIDTH), F32)

    @pl.when(s > 0)
    def _():
        ext_ref[0:CONV_HALO, :] = ext_ref[tm:tm + CONV_HALO, :]

    glu = jnp.dot(h, wglu_ref[...], preferred_element_type=F32)
    ext_ref[CONV_HALO:CONV_HALO + tm, :] = glu[:, :B_WIDTH] * _sigmoid(glu[:, B_WIDTH:])
    zb_all = jnp.dot(h, wzb_ref[...], preferred_element_type=F32)
    first = CONV_HALO - (CONV_K - 1)
    for r in range(tm // row_tile):
        r0 = r * row_tile
        acc = jnp.zeros((row_tile, B_WIDTH), F32)
        for res in range(SUBLANES):
            part = jnp.zeros((row_tile + SUBLANES, B_WIDTH), F32)
            for off in range(res, first + CONV_K, SUBLANES):
                if off >= first:
                    tap = cw_ref[(off - first) * SUBLANES:(off - first + 1) * SUBLANES, :]
                    part = part + (ext_ref[r0 + off - res:r0 + off - res + row_tile + SUBLANES, :]
                                   * jnp.concatenate([tap] * (row_tile // SUBLANES + 1), axis=0))
            acc = acc + part[res:res + row_tile]
        acc = acc + cb_ref[...]
        mu = jnp.mean(acc, axis=-1, keepdims=True)
        xc = acc - mu
        var = jnp.mean(xc * xc, axis=-1, keepdims=True)
        y = xc * lax.rsqrt(var + EPS) * lg_ref[...] + lb_ref[...]
        zb = zb_all[r0:r0 + row_tile]
        cbr_ref[r0:r0 + row_tile, :] = ((y * _sigmoid(y)) * (zb * _sigmoid(zb))).astype(BF16)

    nat_ref[...] = jnp.dot(h, wnat_ref[...], preferred_element_type=F32).astype(BF16)
    ew_ref[...] = jnp.dot(h, wew_ref[...], preferred_element_type=F32)
    vt = lax.dot_general(wvt_ref[...], h, DN_LAST, preferred_element_type=F32).astype(BF16)
    for c in range(vt_ref.shape[0]):
        vt_ref[c] = vt[:, c * LANES:(c + 1) * LANES]
    wt_ref[...] = lax.dot_general(wwt_ref[...], h, DN_LAST, preferred_element_type=F32)


def _proj(x, g, wnat, wglu, wzb, wew, wvt, wwt, cw, cb, lg, lb, tm):
    bsz, seq, d = x.shape
    n_nat, n_ew = wnat.shape[1], wew.shape[1]
    full = lambda a: pl.BlockSpec(a.shape, lambda b, s: (0,) * a.ndim)
    return pl.pallas_call(
        functools.partial(_proj_kernel, row_tile=64),
        grid=(bsz, seq // tm),
        in_specs=[pl.BlockSpec((None, tm, d), lambda b, s: (b, s, 0)),
                  full(g), full(wnat), full(wglu), full(wzb), full(wew), full(wvt), full(wwt),
                  full(cw), full(cb), full(lg), full(lb)],
        out_specs=[pl.BlockSpec((None, tm, n_nat), lambda b, s: (b, s, 0)),
                   pl.BlockSpec((None, tm, n_ew), lambda b, s: (b, s, 0)),
                   pl.BlockSpec((None, tm // LANES, A_WIDTH, LANES), lambda b, s: (b, s, 0, 0)),
                   pl.BlockSpec((None, 16, tm), lambda b, s: (b, 0, s)),
                   pl.BlockSpec((None, tm, B_WIDTH), lambda b, s: (b, s, 0))],
        out_shape=[jax.ShapeDtypeStruct((bsz, seq, n_nat), BF16),
                   jax.ShapeDtypeStruct((bsz, seq, n_ew), F32),
                   jax.ShapeDtypeStruct((bsz, seq // LANES, A_WIDTH, LANES), BF16),
                   jax.ShapeDtypeStruct((bsz, 16, seq), F32),
                   jax.ShapeDtypeStruct((bsz, seq, B_WIDTH), BF16)],
        scratch_shapes=[pltpu.VMEM((tm + CONV_HALO + SUBLANES, B_WIDTH), F32)],
        compiler_params=pltpu.CompilerParams(dimension_semantics=("arbitrary", "arbitrary"),
                                             vmem_limit_bytes=VMEM_LIMIT),
        name="in_proj_conv",
    )(x, g, wnat, wglu, wzb, wew, wvt, wwt, cw, cb, lg, lb)


def _bit_planes(words):
    a = list(words)
    j, m = 16, 0x0000FFFF
    while j:
        k = 0
        while k < 32:
            t = (a[k] ^ (a[k + j] >> np.uint32(j))) & np.uint32(m)
            a[k] = a[k] ^ t
            a[k + j] = a[k + j] ^ (t << np.uint32(j))
            k = (k + j + 1) & ~j
        j >>= 1
        m ^= (m << j) & 0xFFFFFFFF
    return a


def _attn_kernel(relb_ref, bucket_ref, q_ref, qi_ref, wt_ref, kidx_ref, k_ref, vt_ref, out_ref,
                 key_scr, plane_scr, mb_scr, mbw_scr, bias_scr, bidx_scr, bq_scr, sa_scr, sb_scr,
                 ca_scr, cb_scr, m_scr, acc_scr, ot_scr, *, k_top, idx_bits):
    i = pl.program_id(1)
    n_chunks = i // (KEY_CHUNK // TQ) + 1
    idx_scale = (IDX_DIM ** -0.5) * (IDX_HEADS ** -0.5)
    attn_scale = A_HEAD_DIM ** -0.5

    @pl.when(i == 0)
    def _init_bias():
        plane_scr[...] = jnp.zeros(plane_scr.shape, U32)
        for h in range(A_HEADS):
            bias_scr[2, h] = jnp.zeros((TQ, LANES), F32)
        for d in range(2):
            for h in range(A_HEADS):
                bias_scr[d, h] = jnp.zeros((TQ, LANES), F32) + relb_ref[0, h]

        def fill(b, carry):
            for d in range(2):
                hit = bucket_ref[d] == b
                for h in range(A_HEADS):
                    bias_scr[d, h] = jnp.where(hit, relb_ref[b, h], bias_scr[d, h])
            return carry

        lax.fori_loop(1, REL_BUCKETS, fill, 0)
        for d in range(2):
            for h in range(A_HEADS):
                bias_scr[d, h] = (bias_scr[d, h] - relb_ref[FAR_BUCKET, h]) * LOG2E
        eye = (lax.broadcasted_iota(I32, (2 * TQ, LANES), 0) & (TQ - 1)) == lax.broadcasted_iota(
            I32, (2 * TQ, LANES), 1)
        for j in range(A_HEADS // 2):
            bq_scr[j, :, LANES:] = jnp.where(eye, 1.0, 0.0).astype(BF16)

    lane = lax.broadcasted_iota(I32, (TQ, LANES), 1)
    low = lane < A_HEAD_DIM
    for j in range(A_HEADS // 2):
        qi = qi_ref[:, j * LANES:(j + 1) * LANES].astype(F32)
        bidx_scr[j, 0:TQ, :] = jnp.where(low, qi, 0.0).astype(BF16)
        bidx_scr[j, TQ:2 * TQ, :] = jnp.where(low, 0.0, qi).astype(BF16)
        qq = q_ref[:, j * LANES:(j + 1) * LANES].astype(F32) * (attn_scale * LOG2E)
        bq_scr[j, 0:TQ, 0:LANES] = jnp.where(low, qq, 0.0).astype(BF16)
        bq_scr[j, TQ:2 * TQ, 0:LANES] = jnp.where(low, 0.0, qq).astype(BF16)

    q_chunk = (i * TQ + lax.broadcasted_iota(I32, (1, LANES), 1)) >> CHUNK_SHIFT
    w_scaled = [wt_ref[h:h + 1, :] * idx_scale for h in range(IDX_HEADS)]
    row_iota = lax.broadcasted_iota(I32, (KEY_CHUNK, LANES), 0)

    def score_rows(base, rows, masked):
        kc = kidx_ref[pl.ds(base, rows), :]
        acc = jnp.zeros((rows, LANES), F32)
        for j in range(IDX_HEADS // 2):
            d = lax.dot_general(kc, bidx_scr[j], DN_LAST, preferred_element_type=F32)
            acc = acc + jnp.maximum(d[:, :TQ], 0.0) * w_scaled[2 * j]
            acc = acc + jnp.maximum(d[:, TQ:], 0.0) * w_scaled[2 * j + 1]
        if masked:
            acc = jnp.where(((base + row_iota[0:rows]) >> CHUNK_SHIFT) <= q_chunk, acc, -jnp.inf)
        bits = lax.bitcast_convert_type(acc, I32)
        ukey = lax.bitcast_convert_type(bits ^ ((bits >> 31) | INT_MIN), U32)
        key_scr[pl.ds(base, rows), :] = lax.bitcast_convert_type(ukey ^ np.uint32(0x80000000), I32)
        for gg in range(rows // GROUP_ROWS):
            words = [ukey[gg * GROUP_ROWS + w * SUBLANES:gg * GROUP_ROWS + (w + 1) * SUBLANES] for w in range(32)]
            grp = pl.ds(pl.multiple_of(base // 32 + gg * SUBLANES, SUBLANES), SUBLANES)
            for b, plane in enumerate(_bit_planes(words)):
                plane_scr[b, grp, :] = plane

    def score_chunk(ci):
        score_rows(pl.multiple_of(ci * KEY_CHUNK, KEY_CHUNK), KEY_CHUNK, masked=False)

    def score_pair(t, carry):
        score_chunk(2 * t)
        score_chunk(2 * t + 1)
        return carry

    lax.fori_loop(0, (n_chunks - 1) // 2, score_pair, 0)

    @pl.when(lax.rem(n_chunks - 1, 2) == 1)
    def _():
        score_chunk(n_chunks - 2)

    last0 = pl.multiple_of((n_chunks - 1) * KEY_CHUNK, KEY_CHUNK)
    half = KEY_CHUNK // 2
    second_half = lax.rem(i, KEY_CHUNK // TQ) >= half // TQ

    @pl.when(second_half)
    def _():
        score_rows(last0, KEY_CHUNK, masked=True)

    @pl.when(jnp.logical_not(second_half))
    def _():
        score_rows(last0, half, masked=True)
        key_scr[pl.ds(pl.multiple_of(last0 + half, half), half), :] = jnp.full((half, LANES), KEY_NEG_INF, I32)

    n_groups = (n_chunks - 1) * (KEY_CHUNK // GROUP_ROWS) + jnp.where(second_half, 2, 1) * (half // GROUP_ROWS)
    max_groups = plane_scr.shape[1] // SUBLANES

    def lane_total(words):
        pc = [lax.population_count(w) for w in words]
        while len(pc) > 1:
            pc = [pc[k] + pc[k + 1] for k in range(0, len(pc) - 1, 2)] + ([pc[-1]] if len(pc) % 2 else [])
        return jnp.sum(lax.bitcast_convert_type(pc[0], I32), axis=0, keepdims=True)

    def choose_bit(b, carry):
        alive, need, prefix = carry
        planes = plane_scr[b]
        ones = [alive[g] & planes[g * SUBLANES:(g + 1) * SUBLANES] for g in range(max_groups)]
        cnt = lane_total(ones)
        take = cnt >= need
        alive = tuple(jnp.where(take, ones[g], alive[g] ^ ones[g]) for g in range(max_groups))
        bit = lax.shift_left(jnp.uint32(1), jnp.asarray(31 - b, U32))
        return alive, jnp.where(take, need, need - cnt), jnp.where(take, prefix | bit, prefix)

    alive0 = tuple(jnp.zeros((SUBLANES, LANES), U32) + jnp.where(g < n_groups, np.uint32(0xFFFFFFFF), np.uint32(0))
                   for g in range(max_groups))
    alive, need_left, prefix = lax.fori_loop(
        0, 32, choose_bit, (alive0, jnp.full((1, LANES), k_top, I32), jnp.zeros((1, LANES), U32)))
    thr = lax.bitcast_convert_type(prefix ^ np.uint32(0x80000000), I32)
    n_tied = lane_total(list(alive))

    tie = (n_tied > need_left) & (thr > KEY_NEG_INF) & (thr < KEY_POS_INF)
    any_tie = jnp.max(tie.astype(I32)) > 0

    def write_mask(select):
        def body(ci, carry):
            base = pl.multiple_of(ci * KEY_CHUNK, KEY_CHUNK)
            blk = key_scr[pl.ds(base, KEY_CHUNK), :]
            sel = select(blk, base) & (blk > KEY_NEG_INF) & (blk < KEY_POS_INF)
            mb_scr[pl.ds(base, KEY_CHUNK), :] = jnp.where(sel, 0.0, MASKED).astype(BF16)
            return carry
        lax.fori_loop(0, n_chunks, body, 0)

    @pl.when(jnp.logical_not(any_tie))
    def _():
        write_mask(lambda blk, base: blk >= thr)

    @pl.when(any_tie)
    def _break_ties():
        sub = lax.broadcasted_iota(I32, (SUBLANES, LANES), 0)
        all_ones = np.uint32(0xFFFFFFFF)

        def tied_below(row):
            parts = []
            for g in range(max_groups):
                r = row - g * GROUP_ROWS
                words = jnp.clip(r >> 3, 0, 32)
                top = ~lax.shift_right_logical(jnp.zeros((1, LANES), U32) + all_ones,
                                               jnp.minimum(words, 31).astype(U32))
                top = jnp.where(words >= 32, all_ones, top)
                edge_bit = lax.shift_left(jnp.zeros((1, LANES), U32) + np.uint32(1),
                                          (31 - jnp.minimum(words, 31)).astype(U32))
                edge = (sub < (r & (SUBLANES - 1))) & (r >= 0) & (words < 32)
                parts.append(alive[g] & (top | jnp.where(edge, edge_bit, np.uint32(0))))
            return lane_total(parts)

        def grow(p, last):
            trial = last + lax.shift_left(jnp.int32(1), idx_bits - 1 - p)
            return jnp.where(tied_below(trial) <= need_left - 1, trial, last)

        last = lax.fori_loop(0, idx_bits, grow, jnp.zeros((1, LANES), I32))
        last = jnp.where(tie, last, INT_MAX)
        write_mask(lambda blk, base: (blk > thr) | ((blk == thr) & ((base + row_iota) <= last)))

    blocks_per_chunk = KEY_CHUNK // TQ
    win_blk = jnp.maximum(i - (blocks_per_chunk - 1), 0)
    win0 = pl.multiple_of(win_blk * TQ, TQ)
    mbw_scr[...] = mb_scr[pl.ds(win0, KEY_CHUNK), :]
    mb_scr[pl.ds(win0, KEY_CHUNK), :] = jnp.full((KEY_CHUNK, LANES), MASKED, BF16)
    n_far = (win_blk + blocks_per_chunk - 1) // blocks_per_chunk

    def window_bias(h):
        tiles = []
        for t in range(blocks_per_chunk):
            d = i - (win_blk + t)
            tiles.append(bias_scr[jnp.where(d == 0, 0, jnp.where(d == 1, 1, 2)), h])
        return jnp.concatenate(tiles, axis=0)

    m_scr[...] = jnp.full((A_HEADS, TQ), M_INIT, F32)
    acc_scr[...] = jnp.zeros((A_HEADS, ACC_ROWS, TQ), F32)
    ones = jnp.ones((ACC_ROWS - A_HEAD_DIM, KEY_CHUNK), BF16)

    def stage_a(row0, mask, bias_of, s_buf, c_buf):
        for j in range(A_HEADS // 2):
            kc = jnp.concatenate([k_ref[pl.ds(row0, KEY_CHUNK), j * LANES:(j + 1) * LANES], mask], axis=1)
            s = lax.dot_general(kc, bq_scr[j], DN_LAST, preferred_element_type=F32)
            for hh in range(2):
                h = 2 * j + hh
                sh = s[:, hh * TQ:(hh + 1) * TQ]
                if bias_of is not None:
                    sh = sh + bias_of(h)
                s_buf[h] = sh
                c_buf[h:h + 1, :] = jnp.max(sh, axis=0, keepdims=True)

    def stage_a_far(ci, s_buf, c_buf):
        row0 = pl.multiple_of(ci * KEY_CHUNK, KEY_CHUNK)
        stage_a(row0, mb_scr[pl.ds(row0, KEY_CHUNK), :], None, s_buf, c_buf)

    def stage_b(blk0, s_buf, c_buf):
        for h in range(A_HEADS):
            m_old = m_scr[h:h + 1, :]
            m_new = jnp.maximum(m_old, c_buf[h:h + 1, :])
            p = jnp.exp2(s_buf[h] - m_new).astype(BF16)
            vt = [vt_ref[blk0 + t, h * A_HEAD_DIM:(h + 1) * A_HEAD_DIM, :] for t in range(blocks_per_chunk)]
            pv = jnp.dot(jnp.concatenate([jnp.concatenate(vt, axis=1), ones], axis=0), p,
                         preferred_element_type=F32)
            acc_scr[h] = jnp.exp2(m_old - m_new) * acc_scr[h] + pv
            m_scr[h:h + 1, :] = m_new

    stage_a(win0, mbw_scr[...], window_bias, sa_scr, ca_scr)

    def item_pair(t, carry):
        stage_a_far(2 * t, sb_scr, cb_scr)
        stage_b(jnp.where(t == 0, win_blk, (2 * t - 1) * blocks_per_chunk), sa_scr, ca_scr)

        @pl.when(2 * t + 1 < n_far)
        def _():
            stage_a_far(2 * t + 1, sa_scr, ca_scr)
            stage_b(2 * t * blocks_per_chunk, sb_scr, cb_scr)

        return carry

    lax.fori_loop(0, (n_far + 1) // 2, item_pair, 0)
    last_blk = jnp.where(n_far == 0, win_blk, (n_far - 1) * blocks_per_chunk)

    @pl.when(lax.rem(n_far, 2) == 1)
    def _():
        stage_b(last_blk, sb_scr, cb_scr)

    @pl.when(lax.rem(n_far, 2) == 0)
    def _():
        stage_b(last_blk, sa_scr, ca_scr)

    for h in range(A_HEADS):
        ot_scr[h * A_HEAD_DIM:(h + 1) * A_HEAD_DIM, :] = (
            acc_scr[h, 0:A_HEAD_DIM, :] / acc_scr[h, A_HEAD_DIM:A_HEAD_DIM + 1, :])

    out_ref[...] = ot_scr[...].T


def _t5_bucket(rel):
    half = REL_BUCKETS // 2
    max_exact = half // 2
    base = jnp.where(rel > 0, half, 0).astype(jnp.int32)
    n = jnp.abs(rel)
    nf = jnp.maximum(n, 1).astype(jnp.float32)
    large = max_exact + (jnp.log(nf / max_exact) / math.log(REL_MAX_DIST / max_exact)
                         * (half - max_exact)).astype(jnp.int32)
    large = jnp.minimum(large, half - 1)
    return base + jnp.where(n < max_exact, n, large)


def _attention(nat, wt, vt, rel_bias):
    bsz, seq, _ = nat.shape
    k_top = min(TOPK_MAX, seq // 4)
    idx_bits = max(1, int(math.ceil(math.log2(seq))))
    r = jnp.arange(TQ, dtype=jnp.int32)[:, None]
    t = jnp.arange(TQ, dtype=jnp.int32)[None, :]
    buckets = jnp.stack([_t5_bucket(r - t - TQ * d) for d in range(2)])
    return pl.pallas_call(
        functools.partial(_attn_kernel, k_top=k_top, idx_bits=idx_bits),
        grid=(bsz, seq // TQ),
        in_specs=[pl.BlockSpec(memory_space=pltpu.SMEM),
                  pl.BlockSpec((2, TQ, LANES), lambda b, i: (0, 0, 0)),
                  pl.BlockSpec((None, TQ, A_WIDTH), lambda b, i: (b, i, 0)),
                  pl.BlockSpec((None, TQ, A_WIDTH), lambda b, i: (b, i, 2)),
                  pl.BlockSpec((None, 16, TQ), lambda b, i: (b, 0, i)),
                  pl.BlockSpec((None, seq, LANES), lambda b, i: (b, 0, 3 * A_WIDTH // LANES)),
                  pl.BlockSpec((None, seq, A_WIDTH), lambda b, i: (b, 0, 1)),
                  pl.BlockSpec((None, seq // LANES, A_WIDTH, LANES), lambda b, i: (b, 0, 0, 0))],
        out_specs=pl.BlockSpec((None, TQ, A_WIDTH), lambda b, i: (b, i, 0)),
        out_shape=jax.ShapeDtypeStruct((bsz, seq, A_WIDTH), F32),
        scratch_shapes=[pltpu.VMEM((seq, LANES), I32),
                        pltpu.VMEM((32, seq // 32, LANES), U32),
                        pltpu.VMEM((seq, LANES), BF16),
                        pltpu.VMEM((KEY_CHUNK, LANES), BF16),
                        pltpu.VMEM((3, A_HEADS, TQ, LANES), F32),
                        pltpu.VMEM((A_HEADS // 2, 2 * TQ, LANES), BF16),
                        pltpu.VMEM((A_HEADS // 2, 2 * TQ, 2 * LANES), BF16),
                        pltpu.VMEM((A_HEADS, KEY_CHUNK, TQ), F32),
                        pltpu.VMEM((A_HEADS, KEY_CHUNK, TQ), F32),
                        pltpu.VMEM((A_HEADS, TQ), F32),
                        pltpu.VMEM((A_HEADS, TQ), F32),
                        pltpu.VMEM((A_HEADS, TQ), F32),
                        pltpu.VMEM((A_HEADS, ACC_ROWS, TQ), F32),
                        pltpu.VMEM((A_WIDTH, TQ), F32)],
        compiler_params=pltpu.CompilerParams(dimension_semantics=("arbitrary", "arbitrary"),
                                             vmem_limit_bytes=VMEM_LIMIT),
        name="sparse_attn",
    )(rel_bias, buckets, nat, nat, wt, nat, nat, vt)


def _out_kernel(attn_ref, za_ref, cb_ref, ga_ref, gb_ref, x_ref, p_ref, wa_ref, wb_ref, wo_ref,
                pg_ref, wpg_ref, wpp_ref, fg_ref, out_ref):
    za = za_ref[...]
    ya_in = (attn_ref[...] * (za * _sigmoid(za))).astype(BF16)
    y_a = jnp.dot(ya_in, wa_ref[...], preferred_element_type=F32)
    y_b = jnp.dot(cb_ref[...], wb_ref[...], preferred_element_type=F32)
    merged = _sigmoid(ga_ref[...]) * y_a + _sigmoid(gb_ref[...]) * y_b
    x1 = x_ref[...] + jnp.dot(merged.astype(BF16), wo_ref[...], preferred_element_type=F32)
    e = jnp.dot(p_ref[...].astype(BF16), wpp_ref[...], preferred_element_type=F32)
    gate = _sigmoid(jnp.dot(_rms(x1, pg_ref[...]).astype(BF16), wpg_ref[...], preferred_element_type=F32))
    out_ref[...] = _rms(x1 + gate * e, fg_ref[...])


def _out_stage(attn, ew, cbr, x, p, wa, wb, wo, pg, wpg, wpp, fg, tm):
    bsz, seq, d = x.shape
    full = lambda a: pl.BlockSpec(a.shape, lambda b, s: (0,) * a.ndim)
    return pl.pallas_call(
        _out_kernel,
        grid=(bsz, seq // tm),
        in_specs=[pl.BlockSpec((None, tm, A_WIDTH), lambda b, s: (b, s, 0)),
                  pl.BlockSpec((None, tm, A_WIDTH), lambda b, s: (b, s, 2 * D_MODEL // A_WIDTH)),
                  pl.BlockSpec((None, tm, B_WIDTH), lambda b, s: (b, s, 0)),
                  pl.BlockSpec((None, tm, D_MODEL), lambda b, s: (b, s, 0)),
                  pl.BlockSpec((None, tm, D_MODEL), lambda b, s: (b, s, 1)),
                  pl.BlockSpec((None, tm, d), lambda b, s: (b, s, 0)),
                  pl.BlockSpec((None, tm, PLE_DIM), lambda b, s: (b, s, 0)),
                  full(wa), full(wb), full(wo), full(pg), full(wpg), full(wpp), full(fg)],
        out_specs=pl.BlockSpec((None, tm, d), lambda b, s: (b, s, 0)),
        out_shape=jax.ShapeDtypeStruct((bsz, seq, d), F32),
        compiler_params=pltpu.CompilerParams(dimension_semantics=("arbitrary", "arbitrary"),
                                             vmem_limit_bytes=VMEM_LIMIT),
        name="merge_out",
    )(attn, ew, cbr, ew, ew, x, p, wa, wb, wo, pg, wpg, wpp, fg)


def kernel(x, p, norm_in_g, w_in, conv_w, conv_b, conv_ln_g, conv_ln_b, w_branch_a, w_branch_b, w_out,
           ple_norm_g, w_ple_gate, w_ple_proj, rel_bias, final_norm_g):
    seq = x.shape[1]
    assert w_in.shape[0] == 1, "one layer: the fused output stage ends with the final norm"
    assert seq % KEY_CHUNK == 0 and seq >= 2 * TQ and x.shape[2] == D_MODEL
    offs = np.cumsum((0,) + IN_SIZES)
    row = lambda v: v.reshape(1, -1)
    for i in range(1):
        w = w_in[i]
        wq, wk, wv, wza, wqi, wki, wwi, wglu, wzb, wga, wgb = [w[:, offs[n]:offs[n + 1]] for n in range(11)]
        wnat = jnp.concatenate([wq, wk, wqi, wki, wki], axis=1).astype(BF16)
        wew = jnp.concatenate([wga, wgb, wza], axis=1).astype(BF16)
        wvt = wv.T.astype(BF16)
        wwt = jnp.pad(wwi.T, ((0, 16 - IDX_HEADS), (0, 0))).astype(BF16)
        nat, ew, vt, wt, cbr = _proj(x, row(norm_in_g[i]), wnat, wglu.astype(BF16), wzb.astype(BF16), wew, wvt,
                                     wwt, jnp.repeat(conv_w[i].reshape(CONV_K, B_WIDTH), SUBLANES, axis=0),
                                     row(conv_b[i]),
                                     row(conv_ln_g[i]), row(conv_ln_b[i]), tm=256)
        attn = _attention(nat, wt, vt, rel_bias)
        x = _out_stage(attn, ew, cbr, x, p[i], w_branch_a[i].astype(BF16), w_branch_b[i].astype(BF16),
                       w_out[i].astype(BF16), row(ple_norm_g[i]), w_ple_gate[i].astype(BF16),
                       w_ple_proj[i].astype(BF16), row(final_norm_g), tm=512)
    return x
```
